```python
import jax, jax.numpy as jnp
from jax import lax
import numpy as np

D_MODEL = 2048
BATCH = 2
SEQ = 8192
DEPTH = 2
DEC_BATCH = 8
DEC_SEQ = 4096
PAST_LEN = 128

RET_WIDTH = D_MODEL // 2
SG_WIDTH = D_MODEL - RET_WIDTH
RET_HEADS = 4
RET_HEAD_DIM = RET_WIDTH // RET_HEADS
RET_CHUNK = 128
SG_GROUPS = 8
SG_GROUP_DIM = SG_WIDTH // SG_GROUPS
SG_CHUNK = 128
IN_WIDTH = 4 * RET_WIDTH + 2 * SG_WIDTH
D_FF = -(-8 * D_MODEL // (3 * 256)) * 256
PLE_DIM = 256
ROPE_BASE = 10000.0
EPS = 1e-6

kernel_name = "hymba_retention_gmlp_encoder"


def rms_norm(x, gain=None):
    xf = x.astype(jnp.float32)
    y = xf * lax.rsqrt(jnp.mean(xf * xf, axis=-1, keepdims=True) + EPS)
    if gain is not None:
        y = y * gain.astype(jnp.float32)
    return y.astype(x.dtype)


def layer_norm(x, gain, bias):
    xf = x.astype(jnp.float32)
    mu = jnp.mean(xf, axis=-1, keepdims=True)
    xc = xf - mu
    y = xc * lax.rsqrt(jnp.mean(xc * xc, axis=-1, keepdims=True) + EPS)
    return (y * gain.astype(jnp.float32) + bias.astype(jnp.float32)).astype(x.dtype)


def rotary(x):
    s, d = x.shape[1], x.shape[-1]
    half = d // 2
    inv_freq = ROPE_BASE ** (-jnp.arange(half, dtype=jnp.float32) / half)
    ang = jnp.arange(s, dtype=jnp.float32)[:, None] * inv_freq[None, :]
    cos = jnp.cos(ang)[None, :, None, :].astype(x.dtype)
    sin = jnp.sin(ang)[None, :, None, :].astype(x.dtype)
    x1, x2 = x[..., :half], x[..., half:]
    return jnp.concatenate([x1 * cos - x2 * sin, x1 * sin + x2 * cos], axis=-1)


def retention_scan(q, k, v, log_gamma, include_diag):
    b, s, h, dk = q.shape
    dv = v.shape[-1]
    n = s // RET_CHUNK
    dt = q.dtype
    idx = jnp.arange(RET_CHUNK, dtype=jnp.float32)
    diff = idx[:, None] - idx[None, :]
    allowed = (diff >= 0) if include_diag else (diff > 0)
    intra = jnp.where(allowed[None], jnp.exp(log_gamma[:, None, None] * jnp.where(allowed, diff, 0.0)[None]), 0.0).astype(dt)
    q_dec = jnp.exp(log_gamma[:, None] * (idx + 1.0)[None, :]).astype(dt)
    k_dec = jnp.exp(log_gamma[:, None] * (RET_CHUNK - 1.0 - idx)[None, :]).astype(dt)
    c_dec = jnp.exp(log_gamma * RET_CHUNK).astype(dt)

    def to_chunks(t):
        return t.reshape(b, n, RET_CHUNK, h, t.shape[-1]).transpose(1, 0, 3, 2, 4)

    def step(state, inp):
        qc, kc, vc = inp
        scores = jnp.einsum('bhid,bhjd->bhij', qc, kc) * intra[None]
        y = (jnp.einsum('bhij,bhje->bhie', scores, vc)
             + jnp.einsum('bhid,bhde->bhie', qc * q_dec[None, :, :, None], state))
        state = (state * c_dec[None, :, None, None]
                 + jnp.einsum('bhjd,bhje->bhde', kc * k_dec[None, :, :, None], vc))
        return state, y

    state0 = jnp.zeros((b, h, dk, dv), dt)
    _, y = lax.scan(step, state0, (to_chunks(q), to_chunks(k), to_chunks(v)))
    return y.transpose(1, 0, 3, 2, 4).reshape(b, s, h, dv)


def retention_mixer(q, k, v, g, log_gamma):
    b, s, _ = q.shape
    heads = lambda t: t.reshape(b, s, RET_HEADS, RET_HEAD_DIM)
    q = rotary(heads(q))
    k = rotary(heads(k)) * (RET_HEAD_DIM ** -0.5)
    v = heads(v)
    y_f = retention_scan(q, k, v, log_gamma[0], True)
    y_b = retention_scan(q[:, ::-1], k[:, ::-1], v[:, ::-1], log_gamma[1], False)[:, ::-1]
    y = rms_norm(y_f + y_b)
    return (jax.nn.silu(heads(g)) * y).reshape(b, s, RET_WIDTH)


def spatial_gating_mixer(u, v, ln_g, ln_b, w_s, b_s):
    b, s, _ = u.shape
    n = s // SG_CHUNK
    u = jax.nn.gelu(u)
    v = jax.nn.gelu(v).reshape(b, s, SG_GROUPS, SG_GROUP_DIM)
    v = layer_norm(v, ln_g.reshape(SG_GROUPS, SG_GROUP_DIM), ln_b.reshape(SG_GROUPS, SG_GROUP_DIM))
    v = v.reshape(b, n, SG_CHUNK, SG_GROUPS, SG_GROUP_DIM)
    z = jnp.einsum('gij,bnjgc->bnigc', w_s, v) + b_s.T[:, :, None]
    return u * z.reshape(b, s, SG_WIDTH)


def trunk(x, p, norm_mix_g, w_in, ret_decay, sg_ln_g, sg_ln_b, sg_w, sg_b, w_out,
          norm_ffn_g, w_ffn_gate, w_ffn_up, w_ffn_down, norm_ple_g, w_ple_gate, w_ple_proj,
          norm_final_g):
    h = x
    for l in range(DEPTH):
        hn = rms_norm(h, norm_mix_g[l])
        proj = hn @ w_in[l]
        q, k, v, g, u, sv = jnp.split(
            proj, [RET_WIDTH, 2 * RET_WIDTH, 3 * RET_WIDTH, 4 * RET_WIDTH, 4 * RET_WIDTH + SG_WIDTH], axis=-1)
        log_gamma = jnp.log1p(-jnp.exp2(-5.0 - ret_decay[l].astype(jnp.float32)))
        y_ret = retention_mixer(q, k, v, g, log_gamma)
        y_sg = spatial_gating_mixer(u, sv, sg_ln_g[l], sg_ln_b[l], sg_w[l], sg_b[l])
        h = h + jnp.concatenate([y_ret, y_sg], axis=-1) @ w_out[l]
        hn = rms_norm(h, norm_ffn_g[l])
        h = h + (jax.nn.silu(hn @ w_ffn_gate[l]) * (hn @ w_ffn_up[l])) @ w_ffn_down[l]
        gate = jax.nn.sigmoid(rms_norm(h, norm_ple_g[l]) @ w_ple_gate[l])
        h = h + (p[l] @ w_ple_proj[l]) * gate
    return rms_norm(h, norm_final_g)


def setup_inputs(seed: int = 0) -> dict:
    key = jax.random.key(seed)
    ks = jax.random.split(key, 24)
    f32 = jnp.float32
    nrm = lambda k, shape, scale: jax.random.normal(k, shape, f32) * scale
    gain = lambda k, shape: 1.0 + 0.02 * jax.random.normal(k, shape, f32)
    base = jnp.linspace(0.0, 7.0, RET_HEADS, dtype=f32)
    return {
        "x_prompt": nrm(ks[0], (BATCH, SEQ, D_MODEL), 1.0),
        "x_sample": nrm(ks[1], (DEC_BATCH, DEC_SEQ, D_MODEL), 1.0),
        "p_prompt": nrm(ks[2], (DEPTH, BATCH, SEQ, PLE_DIM), 1.0),
        "p_sample": nrm(ks[3], (DEPTH, DEC_BATCH, DEC_SEQ, PLE_DIM), 1.0),
        "norm_mix_g": gain(ks[4], (DEPTH, D_MODEL)),
        "w_in": nrm(ks[5], (DEPTH, D_MODEL, IN_WIDTH), D_MODEL ** -0.5),
        "ret_decay": base[None, None, :] + 0.25 * jax.random.normal(ks[6], (DEPTH, 2, RET_HEADS), f32),
        "sg_ln_g": gain(ks[7], (DEPTH, SG_WIDTH)),
        "sg_ln_b": nrm(ks[8], (DEPTH, SG_WIDTH), 0.02),
        "sg_w": nrm(ks[9], (DEPTH, SG_GROUPS, SG_CHUNK, SG_CHUNK), SG_CHUNK ** -0.5),
        "sg_b": gain(ks[10], (DEPTH, SG_GROUPS, SG_CHUNK)),
        "w_out": nrm(ks[11], (DEPTH, D_MODEL, D_MODEL), D_MODEL ** -0.5),
        "norm_ffn_g": gain(ks[12], (DEPTH, D_MODEL)),
        "w_ffn_gate": nrm(ks[13], (DEPTH, D_MODEL, D_FF), D_MODEL ** -0.5),
        "w_ffn_up": nrm(ks[14], (DEPTH, D_MODEL, D_FF), D_MODEL ** -0.5),
        "w_ffn_down": nrm(ks[15], (DEPTH, D_FF, D_MODEL), D_FF ** -0.5),
        "norm_ple_g": gain(ks[16], (DEPTH, D_MODEL)),
        "w_ple_gate": nrm(ks[17], (DEPTH, D_MODEL, D_MODEL), D_MODEL ** -0.5),
        "w_ple_proj": nrm(ks[18], (DEPTH, PLE_DIM, D_MODEL), PLE_DIM ** -0.5),
        "norm_final_g": gain(ks[19], (D_MODEL,)),
    }


def reference(x_prompt, x_sample, p_prompt, p_sample, norm_mix_g, w_in, ret_decay, sg_ln_g, sg_ln_b,
              sg_w, sg_b, w_out, norm_ffn_g, w_ffn_gate, w_ffn_up, w_ffn_down, norm_ple_g, w_ple_gate,
              w_ple_proj, norm_final_g):
    y_prompt = trunk(x_prompt, p_prompt, norm_mix_g, w_in, ret_decay, sg_ln_g, sg_ln_b, sg_w, sg_b, w_out,
                     norm_ffn_g, w_ffn_gate, w_ffn_up, w_ffn_down, norm_ple_g, w_ple_gate, w_ple_proj,
                     norm_final_g)
    y_sample = trunk(x_sample, p_sample, norm_mix_g, w_in, ret_decay, sg_ln_g, sg_ln_b, sg_w, sg_b, w_out,
                     norm_ffn_g, w_ffn_gate, w_ffn_up, w_ffn_down, norm_ple_g, w_ple_gate, w_ple_proj,
                     norm_final_g)
    return (y_prompt, y_sample)
```

```python
import functools

import jax
import jax.numpy as jnp
from jax import lax
from jax.experimental import pallas as pl
from jax.experimental.pallas import tpu as pltpu

RET_HEADS = 4
SG_GROUPS = 8
SG_CHUNK = 128
ROPE_BASE = 10000.0
EPS = 1e-6

RET_TILE = 256

F32 = jnp.float32
BF16 = jnp.bfloat16
V7X_VMEM_BYTES = 64 * 1024 * 1024


def _vmem_limit(estimate_bytes):
    return int(min(V7X_VMEM_BYTES - 6 * 1024 * 1024, estimate_bytes * 1.25 + 4 * 1024 * 1024))


def _rms(x):
    return x * lax.rsqrt(jnp.mean(x * x, axis=-1, keepdims=True) + EPS)


def _sigmoid(x):
    return 1.0 / (1.0 + jnp.exp(-x))


def _in_proj_kernel(h_ref, g_ref, w_ref, cos_ref, sin_ref, lng_ref, lnb_ref, o_ref, hn_ref,
                    *, head_dim, group_dim):
    j = pl.program_id(1)

    @pl.when(j == 0)
    def _():
        hn_ref[...] = (_rms(h_ref[...]) * g_ref[...]).astype(BF16)

    acc = jnp.dot(hn_ref[...], w_ref[...], preferred_element_type=F32)
    width = acc.shape[1]
    half = head_dim // 2

    def rotary(scale):
        c = cos_ref[...]
        s = sin_ref[...]
        for hd in range(width // head_dim):
            lo = hd * head_dim
            x1 = acc[:, lo:lo + half]
            x2 = acc[:, lo + half:lo + head_dim]
            o_ref[:, lo:lo + half] = ((x1 * c - x2 * s) * scale).astype(BF16)
            o_ref[:, lo + half:lo + head_dim] = ((x1 * s + x2 * c) * scale).astype(BF16)

    @pl.when(j == 0)
    def _():
        rotary(1.0)

    @pl.when(j == 1)
    def _():
        rotary(head_dim ** -0.5)

    @pl.when(j == 2)
    def _():
        o_ref[...] = acc.astype(BF16)

    @pl.when(j == 3)
    def _():
        o_ref[...] = (acc * _sigmoid(acc)).astype(BF16)

    @pl.when(j == 4)
    def _():
        o_ref[...] = jax.nn.gelu(acc).astype(BF16)

    @pl.when(j == 5)
    def _():
        for g in range(width // group_dim):
            lo = g * group_dim
            x = jax.nn.gelu(acc[:, lo:lo + group_dim])
            xc = x - jnp.mean(x, axis=-1, keepdims=True)
            y = xc * lax.rsqrt(jnp.mean(xc * xc, axis=-1, keepdims=True) + EPS)
            o_ref[:, lo:lo + group_dim] = (
                y * lng_ref[:, lo:lo + group_dim] + lnb_ref[:, lo:lo + group_dim]).astype(BF16)


def _in_proj(h, gain, w, cos, sin, ln_g, ln_b, *, seq, tm):
    t, d = h.shape
    n = w.shape[1]
    seg = n // 6
    head_dim = seg // RET_HEADS
    group_dim = seg // SG_GROUPS
    pos_blocks = seq // tm
    est = (2 * tm * d * 4 + 2 * d * seg * 2 + 2 * tm * seg * 2 + tm * d * 2
           + 4 * tm * (head_dim // 2) * 4 + 3 * tm * seg * 4)
    return pl.pallas_call(
        functools.partial(_in_proj_kernel, head_dim=head_dim, group_dim=group_dim),
        grid=(t // tm, 6),
        in_specs=[
            pl.BlockSpec((tm, d), lambda i, j: (i, 0)),
            pl.BlockSpec((1, d), lambda i, j: (0, 0)),
            pl.BlockSpec((d, seg), lambda i, j: (0, j)),
            pl.BlockSpec((tm, head_dim // 2), lambda i, j: (i % pos_blocks, 0)),
            pl.BlockSpec((tm, head_dim // 2), lambda i, j: (i % pos_blocks, 0)),
            pl.BlockSpec((1, seg), lambda i, j: (0, 0)),
            pl.BlockSpec((1, seg), lambda i, j: (0, 0)),
        ],
        out_specs=pl.BlockSpec((tm, seg), lambda i, j: (i, j)),
        out_shape=jax.ShapeDtypeStruct((t, n), BF16),
        scratch_shapes=[pltpu.VMEM((tm, d), BF16)],
        compiler_params=pltpu.CompilerParams(
            dimension_semantics=("arbitrary", "arbitrary"), vmem_limit_bytes=_vmem_limit(est)),
        name="in_proj",
    )(h, gain, w, cos, sin, ln_g, ln_b)


def _ret_state_kernel(lg_ref, kf_ref, vf_ref, kb_ref, vb_ref, sf_ref, sb_ref, st_ref, *, head_dim):
    t = pl.program_id(1)
    c = kf_ref.shape[0]

    @pl.when(t == 0)
    def _():
        st_ref[...] = jnp.zeros_like(st_ref)

    pos = lax.broadcasted_iota(jnp.int32, (c, 1), 0).astype(F32)
    for hd in range(RET_HEADS):
        sl = slice(hd * head_dim, (hd + 1) * head_dim)
        for d, (k_ref, v_ref, s_ref) in enumerate(((kf_ref, vf_ref, sf_ref), (kb_ref, vb_ref, sb_ref))):
            lg = lg_ref[d, hd]
            k_dec = jnp.exp(lg * ((c - 1.0 - pos) if d == 0 else pos))
            c_dec = jnp.exp(jnp.full((1, head_dim), lg * c, F32))
            s_old = st_ref[d, hd]
            s_ref[hd] = s_old.astype(BF16)
            kd = (k_ref[:, sl].astype(F32) * k_dec).astype(BF16)
            u = lax.dot_general(kd, v_ref[:, sl], (((0,), (0,)), ((), ())),
                                preferred_element_type=F32)
            st_ref[d, hd] = s_old * c_dec + u


def _ret_state(proj3, lg, *, seg):
    b, s, _ = proj3.shape
    c = RET_TILE
    nc = s // c
    head_dim = seg // RET_HEADS
    state = jax.ShapeDtypeStruct((b, nc, RET_HEADS, head_dim, head_dim), BF16)
    state_spec_f = pl.BlockSpec((None, None, RET_HEADS, head_dim, head_dim), lambda i, t: (i, t, 0, 0, 0))
    state_spec_b = pl.BlockSpec((None, None, RET_HEADS, head_dim, head_dim),
                                lambda i, t: (i, nc - 1 - t, 0, 0, 0))
    est = 8 * c * seg * 2 + 4 * RET_HEADS * head_dim * head_dim * 2 + 2 * RET_HEADS * head_dim * head_dim * 4
    return pl.pallas_call(
        functools.partial(_ret_state_kernel, head_dim=head_dim),
        grid=(b, nc),
        in_specs=[
            pl.BlockSpec(memory_space=pltpu.SMEM),
            pl.BlockSpec((None, c, seg), lambda i, t: (i, t, 1)),
            pl.BlockSpec((None, c, seg), lambda i, t: (i, t, 2)),
            pl.BlockSpec((None, c, seg), lambda i, t: (i, nc - 1 - t, 1)),
            pl.BlockSpec((None, c, seg), lambda i, t: (i, nc - 1 - t, 2)),
        ],
        out_specs=[state_spec_f, state_spec_b],
        out_shape=[state, state],
        scratch_shapes=[pltpu.VMEM((2, RET_HEADS, head_dim, head_dim), F32)],
        compiler_params=pltpu.CompilerParams(
            dimension_semantics=("arbitrary", "arbitrary"), vmem_limit_bytes=_vmem_limit(est)),
        name="ret_state",
    )(lg, proj3, proj3, proj3, proj3)


def _mix_out_kernel(lg_ref, p_ref, sf_ref, sb_ref, h_ref, sgw_ref, sgb_ref, wo_ref, o_ref, y_ref,
                    *, seg, head_dim, group_dim):
    c = p_ref.shape[0]
    row = lax.broadcasted_iota(jnp.int32, (c, c), 0)
    col = lax.broadcasted_iota(jnp.int32, (c, c), 1)
    dist = (row - col).astype(F32)
    causal = row >= col
    pos = lax.broadcasted_iota(jnp.int32, (c, 1), 0).astype(F32)

    for hd in range(RET_HEADS):
        lo = hd * head_dim
        lgf = lg_ref[0, hd]
        lgb = lg_ref[1, hd]
        q = p_ref[:, lo:lo + head_dim]
        k = p_ref[:, seg + lo:seg + lo + head_dim]
        v = p_ref[:, 2 * seg + lo:2 * seg + lo + head_dim]
        gate = p_ref[:, 3 * seg + lo:3 * seg + lo + head_dim]
        scores = lax.dot_general(q, k, (((1,), (1,)), ((), ())), preferred_element_type=F32)
        decay = jnp.exp(jnp.where(causal, lgf, -lgb) * dist)
        qf = q.astype(F32)
        lhs = jnp.concatenate([
            (scores * decay).astype(BF16),
            (qf * jnp.exp(lgf * (pos + 1.0))).astype(BF16),
            (qf * jnp.exp(lgb * (c - pos))).astype(BF16)], axis=1)
        rhs = jnp.concatenate([v, sf_ref[hd], sb_ref[hd]], axis=0)
        y = jnp.dot(lhs, rhs, preferred_element_type=F32)
        y_ref[:, lo:lo + head_dim] = (gate.astype(F32) * _rms(y)).astype(BF16)

    for g in range(SG_GROUPS):
        lo = g * group_dim
        u = p_ref[:, 4 * seg + lo:4 * seg + lo + group_dim]
        vn = p_ref[:, 5 * seg + lo:5 * seg + lo + group_dim]
        n_sub = c // SG_CHUNK
        vcat = jnp.concatenate([vn[n * SG_CHUNK:(n + 1) * SG_CHUNK] for n in range(n_sub)], axis=1)
        z = jnp.dot(sgw_ref[g], vcat, preferred_element_type=F32)
        bias = sgb_ref[:, lo:lo + group_dim]
        for n in range(n_sub):
            zn = z[:, n * group_dim:(n + 1) * group_dim] + bias
            un = u[n * SG_CHUNK:(n + 1) * SG_CHUNK].astype(F32)
            y_ref[n * SG_CHUNK:(n + 1) * SG_CHUNK, seg + lo:seg + lo + group_dim] = (un * zn).astype(BF16)

    o_ref[...] = h_ref[...] + jnp.dot(y_ref[...], wo_ref[...], preferred_element_type=F32)


def _mix_out(proj3, sf, sb, h3, lg, sg_w, sg_bias, w_out, *, seg):
    b, s, n = proj3.shape
    d = h3.shape[-1]
    c = RET_TILE
    nc = s // c
    head_dim = seg // RET_HEADS
    group_dim = seg // SG_GROUPS
    state_spec = pl.BlockSpec((None, None, RET_HEADS, head_dim, head_dim), lambda i, t: (i, t, 0, 0, 0))
    est = (2 * c * n * 2 + 4 * RET_HEADS * head_dim * head_dim * 2 + 4 * c * d * 4
           + 2 * d * d * 2 + 2 * SG_CHUNK * seg * 4 + c * d * 2 + 8 * c * c * 4)
    return pl.pallas_call(
        functools.partial(_mix_out_kernel, seg=seg, head_dim=head_dim, group_dim=group_dim),
        grid=(b, nc),
        in_specs=[
            pl.BlockSpec(memory_space=pltpu.SMEM),
            pl.BlockSpec((None, c, n), lambda i, t: (i, t, 0)),
            state_spec,
            state_spec,
            pl.BlockSpec((None, c, d), lambda i, t: (i, t, 0)),
            pl.BlockSpec((SG_GROUPS, SG_CHUNK, SG_CHUNK), lambda i, t: (0, 0, 0)),
            pl.BlockSpec((SG_CHUNK, seg), lambda i, t: (0, 0)),
            pl.BlockSpec((d, d), lambda i, t: (0, 0)),
        ],
        out_specs=pl.BlockSpec((None, c, d), lambda i, t: (i, t, 0)),
        out_shape=jax.ShapeDtypeStruct((b, s, d), F32),
        scratch_shapes=[pltpu.VMEM((c, d), BF16)],
        compiler_params=pltpu.CompilerParams(
            dimension_semantics=("arbitrary", "arbitrary"), vmem_limit_bytes=_vmem_limit(est)),
        name="mix_out",
    )(lg, proj3, sf, sb, h3, sg_w, sg_bias, w_out)


def _ffn_kernel(h_ref, g_ref, wg_ref, wu_ref, wd_ref, o_ref, hn_ref):
    @pl.when(pl.program_id(1) == 0)
    def _():
        x = h_ref[...]
        hn_ref[...] = (_rms(x) * g_ref[...]).astype(BF16)
        o_ref[...] = x

    hn = hn_ref[...]
    gate = jnp.dot(hn, wg_ref[...], preferred_element_type=F32)
    up = jnp.dot(hn, wu_ref[...], preferred_element_type=F32)
    act = (gate * _sigmoid(gate) * up).astype(BF16)
    o_ref[...] += jnp.dot(act, wd_ref[...], preferred_element_type=F32)


def _ffn(h, gain, w_gate, w_up, w_down, *, tm, tf):
    t, d = h.shape
    dff = w_gate.shape[1]
    est = 4 * tm * d * 4 + tm * d * 2 + 2 * 3 * d * tf * 2 + 3 * tm * tf * 4
    return pl.pallas_call(
        _ffn_kernel,
        grid=(t // tm, dff // tf),
        in_specs=[
            pl.BlockSpec((tm, d), lambda i, j: (i, 0)),
            pl.BlockSpec((1, d), lambda i, j: (0, 0)),
            pl.BlockSpec((d, tf), lambda i, j: (0, j)),
            pl.BlockSpec((d, tf), lambda i, j: (0, j)),
            pl.BlockSpec((tf, d), lambda i, j: (j, 0)),
        ],
        out_specs=pl.BlockSpec((tm, d), lambda i, j: (i, 0)),
        out_shape=jax.ShapeDtypeStruct((t, d), F32),
        scratch_shapes=[pltpu.VMEM((tm, d), BF16)],
        compiler_params=pltpu.CompilerParams(
            dimension_semantics=("arbitrary", "arbitrary"), vmem_limit_bytes=_vmem_limit(est)),
        name="ffn",
    )(h, gain, w_gate, w_up, w_down)


def _ple_kernel(h_ref, p_ref, g_ref, wg_ref, wp_ref, gf_ref, o_ref, *, final):
    x = h_ref[...]
    hn = (_rms(x) * g_ref[...]).astype(BF16)
    gate = _sigmoid(jnp.dot(hn, wg_ref[...], preferred_element_type=F32))
    emb = jnp.dot(p_ref[...].astype(BF16), wp_ref[...], preferred_element_type=F32)
    y = x + emb * gate
    if final:
        y = _rms(y) * gf_ref[...]
    o_ref[...] = y


def _ple(h, p, gain, w_gate, w_proj, gain_final, *, tm, final):
    t, d = h.shape
    pd = p.shape[1]
    est = 4 * tm * d * 4 + 2 * tm * pd * 4 + 2 * d * d * 2 + 2 * pd * d * 2 + 4 * tm * d * 4
    return pl.pallas_call(
        functools.partial(_ple_kernel, final=final),
        grid=(t // tm,),
        in_specs=[
            pl.BlockSpec((tm, d), lambda i: (i, 0)),
            pl.BlockSpec((tm, pd), lambda i: (i, 0)),
            pl.BlockSpec((1, d), lambda i: (0, 0)),
            pl.BlockSpec((d, d), lambda i: (0, 0)),
            pl.BlockSpec((pd, d), lambda i: (0, 0)),
            pl.BlockSpec((1, d), lambda i: (0, 0)),
        ],
        out_specs=pl.BlockSpec((tm, d), lambda i: (i, 0)),
        out_shape=jax.ShapeDtypeStruct((t, d), F32),
        compiler_params=pltpu.CompilerParams(
            dimension_semantics=("arbitrary",), vmem_limit_bytes=_vmem_limit(est)),
        name="ple",
    )(h, p, gain, w_gate, w_proj, gain_final)


def _tile(total, preferred):
    tile = min(total, preferred)
    assert total % tile == 0, (total, tile)
    return tile


def _trunk(x, p, params, log_gamma):
    b, s, d = x.shape
    depth = p.shape[0]
    t = b * s
    seg = params["w_in"].shape[2] // 6
    head_dim = seg // RET_HEADS
    half = head_dim // 2
    assert s % RET_TILE == 0 and RET_TILE % SG_CHUNK == 0

    inv_freq = ROPE_BASE ** (-jnp.arange(half, dtype=F32) / half)
    ang = jnp.arange(s, dtype=F32)[:, None] * inv_freq[None, :]
    cos = jnp.cos(ang)
    sin = jnp.sin(ang)

    tm_proj = _tile(s, 1024)
    tm_ffn = _tile(t, 1024)
    tf = _tile(params["w_ffn_gate"].shape[2], 256)
    tm_ple = _tile(t, 512)

    h = x.reshape(t, d)
    for l in range(depth):
        proj = _in_proj(h, params["norm_mix_g"][l], params["w_in"][l], cos, sin,
                        params["sg_ln_g"][l], params["sg_ln_b"][l], seq=s, tm=tm_proj)
        proj3 = proj.reshape(b, s, 6 * seg)
        sf, sb = _ret_state(proj3, log_gamma[l], seg=seg)
        h3 = _mix_out(proj3, sf, sb, h.reshape(b, s, d), log_gamma[l], params["sg_w"][l],
                      params["sg_bias"][l], params["w_out"][l], seg=seg)
        h = _ffn(h3.reshape(t, d), params["norm_ffn_g"][l], params["w_ffn_gate"][l],
                 params["w_ffn_up"][l], params["w_ffn_down"][l], tm=tm_ffn, tf=tf)
        h = _ple(h, p[l].reshape(t, -1), params["norm_ple_g"][l], params["w_ple_gate"][l],
                 params["w_ple_proj"][l], params["norm_final_g"], tm=tm_ple, final=(l == depth - 1))
    return h.reshape(b, s, d)


def kernel(x_prompt, x_sample, p_prompt, p_sample, norm_mix_g, w_in, ret_decay, sg_ln_g, sg_ln_b,
           sg_w, sg_b, w_out, norm_ffn_g, w_ffn_gate, w_ffn_up, w_ffn_down, norm_ple_g, w_ple_gate,
           w_ple_proj, norm_final_g):
    depth, d = norm_mix_g.shape
    seg = w_in.shape[2] // 6
    group_dim = seg // SG_GROUPS
    params = {
        "norm_mix_g": norm_mix_g.reshape(depth, 1, d),
        "w_in": w_in.astype(BF16),
        "sg_ln_g": sg_ln_g.reshape(depth, 1, seg),
        "sg_ln_b": sg_ln_b.reshape(depth, 1, seg),
        "sg_w": sg_w.astype(BF16),
        "sg_bias": jnp.repeat(jnp.swapaxes(sg_b, 1, 2), group_dim, axis=2),
        "w_out": w_out.astype(BF16),
        "norm_ffn_g": norm_ffn_g.reshape(depth, 1, d),
        "w_ffn_gate": w_ffn_gate.astype(BF16),
        "w_ffn_up": w_ffn_up.astype(BF16),
        "w_ffn_down": w_ffn_down.astype(BF16),
        "norm_ple_g": norm_ple_g.reshape(depth, 1, d),
        "w_ple_gate": w_ple_gate.astype(BF16),
        "w_ple_proj": w_ple_proj.astype(BF16),
        "norm_final_g": norm_final_g.reshape(1, d),
    }
    log_gamma = jnp.log1p(-jnp.exp2(-5.0 - ret_decay.astype(F32)))
    y_prompt = _trunk(x_prompt, p_prompt, params, log_gamma)
    y_sample = _trunk(x_sample, p_sample, params, log_gamma)
    return (y_prompt, y_sample)
```

```python
import functools

import jax
import jax.numpy as jnp
from jax import lax
from jax.experimental import pallas as pl
from jax.experimental.pallas import tpu as pltpu

RET_HEADS = 4
SG_GROUPS = 8
SG_CHUNK = 128
ROPE_BASE = 10000.0
EPS = 1e-6

RET_TILE = 256
ROW_CHUNK = 256

F32 = jnp.float32
BF16 = jnp.bfloat16
V7X_VMEM_BYTES = 64 * 1024 * 1024


def _vmem_limit(estimate_bytes):
    return int(min(V7X_VMEM_BYTES - 6 * 1024 * 1024, estimate_bytes * 1.25 + 4 * 1024 * 1024))


def _tile(total, preferred):
    tile = min(total, preferred)
    assert total % tile == 0, (total, tile)
    return tile


def _rms(x):
    return x * lax.rsqrt(jnp.mean(x * x, axis=-1, keepdims=True) + EPS)


def _sigmoid(x):
    return 1.0 / (1.0 + jnp.exp(-x))


def _in_proj_kernel(h_ref, g_ref, w_ref, cos_ref, sin_ref, lng_ref, lnb_ref, o_ref, hn_ref,
                    *, head_dim, group_dim, row_chunk):
    tm, width = o_ref.shape
    half = head_dim // 2

    def rotary(acc, rows, scale):
        c = cos_ref[rows, :]
        s = sin_ref[rows, :]
        for hd in range(width // head_dim):
            lo = hd * head_dim
            x1 = acc[:, lo:lo + half]
            x2 = acc[:, lo + half:lo + head_dim]
            o_ref[rows, lo:lo + half] = ((x1 * c - x2 * s) * scale).astype(BF16)
            o_ref[rows, lo + half:lo + head_dim] = ((x1 * s + x2 * c) * scale).astype(BF16)

    def group_norm(acc, rows):
        for g in range(width // group_dim):
            lo = g * group_dim
            x = jax.nn.gelu(acc[:, lo:lo + group_dim])
            xc = x - jnp.mean(x, axis=-1, keepdims=True)
            y = xc * lax.rsqrt(jnp.mean(xc * xc, axis=-1, keepdims=True) + EPS)
            o_ref[rows, lo:lo + group_dim] = (
                y * lng_ref[:, lo:lo + group_dim] + lnb_ref[:, lo:lo + group_dim]).astype(BF16)

    def plain(fn):
        def epilogue(acc, rows):
            o_ref[rows, :] = fn(acc).astype(BF16)
        return epilogue

    epilogues = (
        lambda acc, rows: rotary(acc, rows, 1.0),
        lambda acc, rows: rotary(acc, rows, head_dim ** -0.5),
        plain(lambda acc: acc),
        plain(lambda acc: acc * _sigmoid(acc)),
        plain(jax.nn.gelu),
        group_norm,
    )

    def segment(seg_index, epilogue):
        @pl.when(pl.program_id(1) == seg_index)
        def _():
            for r in range(tm // row_chunk):
                rows = pl.ds(r * row_chunk, row_chunk)
                if seg_index == 0:
                    hn = (_rms(h_ref[rows, :]) * g_ref[...]).astype(BF16)
                    hn_ref[rows, :] = hn
                else:
                    hn = hn_ref[rows, :]
                epilogue(jnp.dot(hn, w_ref[...], preferred_element_type=F32), rows)

    for seg_index, epilogue in enumerate(epilogues):
        segment(seg_index, epilogue)


def _in_proj(h, gain, w, cos, sin, ln_g, ln_b, *, layer, seq, tm):
    t, d = h.shape
    n = w.shape[2]
    seg = n // 6
    head_dim = seg // RET_HEADS
    group_dim = seg // SG_GROUPS
    pos_blocks = seq // tm
    row_chunk = _tile(tm, ROW_CHUNK)
    est = (2 * tm * d * 4 + 2 * d * seg * 2 + 2 * tm * seg * 2 + tm * d * 2
           + 4 * tm * (head_dim // 2) * 4 + 6 * row_chunk * seg * 4)
    return pl.pallas_call(
        functools.partial(_in_proj_kernel, head_dim=head_dim, group_dim=group_dim, row_chunk=row_chunk),
        grid=(t // tm, 6),
        in_specs=[
            pl.BlockSpec((tm, d), lambda i, j: (i, 0)),
            pl.BlockSpec((None, 1, d), lambda i, j: (layer, 0, 0)),
            pl.BlockSpec((None, d, seg), lambda i, j: (layer, 0, j)),
            pl.BlockSpec((tm, head_dim // 2), lambda i, j: (i % pos_blocks, 0)),
            pl.BlockSpec((tm, head_dim // 2), lambda i, j: (i % pos_blocks, 0)),
            pl.BlockSpec((None, 1, seg), lambda i, j: (layer, 0, 0)),
            pl.BlockSpec((None, 1, seg), lambda i, j: (layer, 0, 0)),
        ],
        out_specs=pl.BlockSpec((tm, seg), lambda i, j: (i, j)),
        out_shape=jax.ShapeDtypeStruct((t, n), BF16),
        scratch_shapes=[pltpu.VMEM((tm, d), BF16)],
        compiler_params=pltpu.CompilerParams(
            dimension_semantics=("arbitrary", "arbitrary"), vmem_limit_bytes=_vmem_limit(est)),
        name="in_proj",
    )(h, gain, w, cos, sin, ln_g, ln_b)


def _ret_state_kernel(lg_ref, kf_ref, vf_ref, kb_ref, vb_ref, sf_ref, sb_ref, st_ref,
                      *, layer, head_dim):
    t = pl.program_id(1)
    c = kf_ref.shape[0]

    @pl.when(t == 0)
    def _():
        st_ref[...] = jnp.zeros_like(st_ref)

    pos = lax.broadcasted_iota(jnp.int32, (c, 1), 0).astype(F32)
    for hd in range(RET_HEADS):
        sl = slice(hd * head_dim, (hd + 1) * head_dim)
        for d, (k_ref, v_ref, s_ref) in enumerate(((kf_ref, vf_ref, sf_ref), (kb_ref, vb_ref, sb_ref))):
            lg = lg_ref[layer, d, hd]
            k_dec = jnp.exp(lg * ((c - 1.0 - pos) if d == 0 else pos))
            c_dec = jnp.exp(jnp.full((1, head_dim), lg * c, F32))
            s_old = st_ref[d, hd]
            s_ref[hd] = s_old.astype(BF16)
            kd = (k_ref[:, sl].astype(F32) * k_dec).astype(BF16)
            u = lax.dot_general(kd, v_ref[:, sl], (((0,), (0,)), ((), ())),
                                preferred_element_type=F32)
            st_ref[d, hd] = s_old * c_dec + u


def _ret_state(proj3, lg, *, layer, seg):
    b, s, _ = proj3.shape
    c = RET_TILE
    nc = s // c
    head_dim = seg // RET_HEADS
    state = jax.ShapeDtypeStruct((b, nc, RET_HEADS, head_dim, head_dim), BF16)
    state_spec_f = pl.BlockSpec((None, None, RET_HEADS, head_dim, head_dim), lambda i, t: (i, t, 0, 0, 0))
    state_spec_b = pl.BlockSpec((None, None, RET_HEADS, head_dim, head_dim),
                                lambda i, t: (i, nc - 1 - t, 0, 0, 0))
    est = 8 * c * seg * 2 + 4 * RET_HEADS * head_dim * head_dim * 2 + 2 * RET_HEADS * head_dim * head_dim * 4
    return pl.pallas_call(
        functools.partial(_ret_state_kernel, layer=layer, head_dim=head_dim),
        grid=(b, nc),
        in_specs=[
            pl.BlockSpec(memory_space=pltpu.SMEM),
            pl.BlockSpec((None, c, seg), lambda i, t: (i, t, 1)),
            pl.BlockSpec((None, c, seg), lambda i, t: (i, t, 2)),
            pl.BlockSpec((None, c, seg), lambda i, t: (i, nc - 1 - t, 1)),
            pl.BlockSpec((None, c, seg), lambda i, t: (i, nc - 1 - t, 2)),
        ],
        out_specs=[state_spec_f, state_spec_b],
        out_shape=[state, state],
        scratch_shapes=[pltpu.VMEM((2, RET_HEADS, head_dim, head_dim), F32)],
        compiler_params=pltpu.CompilerParams(
            dimension_semantics=("arbitrary", "arbitrary"), vmem_limit_bytes=_vmem_limit(est)),
        name="ret_state",
    )(lg, proj3, proj3, proj3, proj3)


def _mix_out_kernel(lg_ref, p_ref, sf_ref, sb_ref, h_ref, sgw_ref, sgb_ref, wo_ref, o_ref, y_ref,
                    *, layer, seg, head_dim, group_dim):
    c = p_ref.shape[0]
    row = lax.broadcasted_iota(jnp.int32, (c, c), 0)
    col = lax.broadcasted_iota(jnp.int32, (c, c), 1)
    dist = (row - col).astype(F32)
    causal = row >= col
    pos = lax.broadcasted_iota(jnp.int32, (c, 1), 0).astype(F32)

    for hd in range(RET_HEADS):
        lo = hd * head_dim
        lgf = lg_ref[layer, 0, hd]
        lgb = lg_ref[layer, 1, hd]
        q = p_ref[:, lo:lo + head_dim]
        k = p_ref[:, seg + lo:seg + lo + head_dim]
        v = p_ref[:, 2 * seg + lo:2 * seg + lo + head_dim]
        gate = p_ref[:, 3 * seg + lo:3 * seg + lo + head_dim]
        scores = lax.dot_general(q, k, (((1,), (1,)), ((), ())), preferred_element_type=F32)
        decay = jnp.exp(jnp.where(causal, lgf, -lgb) * dist)
        qf = q.astype(F32)
        lhs = jnp.concatenate([
            (scores * decay).astype(BF16),
            (qf * jnp.exp(lgf * (pos + 1.0))).astype(BF16),
            (qf * jnp.exp(lgb * (c - pos))).astype(BF16)], axis=1)
        rhs = jnp.concatenate([v, sf_ref[hd], sb_ref[hd]], axis=0)
        y = jnp.dot(lhs, rhs, preferred_element_type=F32)
        y_ref[:, lo:lo + head_dim] = (gate.astype(F32) * _rms(y)).astype(BF16)

    for g in range(SG_GROUPS):
        lo = g * group_dim
        u = p_ref[:, 4 * seg + lo:4 * seg + lo + group_dim]
        vn = p_ref[:, 5 * seg + lo:5 * seg + lo + group_dim]
        n_sub = c // SG_CHUNK
        vcat = jnp.concatenate([vn[n * SG_CHUNK:(n + 1) * SG_CHUNK] for n in range(n_sub)], axis=1)
        z = jnp.dot(sgw_ref[g], vcat, preferred_element_type=F32)
        bias = sgb_ref[:, lo:lo + group_dim]
        for n in range(n_sub):
            zn = z[:, n * group_dim:(n + 1) * group_dim] + bias
            un = u[n * SG_CHUNK:(n + 1) * SG_CHUNK].astype(F32)
            y_ref[n * SG_CHUNK:(n + 1) * SG_CHUNK, seg + lo:seg + lo + group_dim] = (un * zn).astype(BF16)

    o_ref[...] = h_ref[...] + jnp.dot(y_ref[...], wo_ref[...], preferred_element_type=F32)


def _mix_out(proj3, sf, sb, h3, lg, sg_w, sg_bias, w_out, *, layer, seg):
    b, s, n = proj3.shape
    d = h3.shape[-1]
    c = RET_TILE
    nc = s // c
    head_dim = seg // RET_HEADS
    group_dim = seg // SG_GROUPS
    state_spec = pl.BlockSpec((None, None, RET_HEADS, head_dim, head_dim), lambda i, t: (i, t, 0, 0, 0))
    est = (2 * c * n * 2 + 4 * RET_HEADS * head_dim * head_dim * 2 + 4 * c * d * 4
           + 2 * d * d * 2 + 2 * SG_CHUNK * seg * 4 + c * d * 2 + 8 * c * c * 4)
    return pl.pallas_call(
        functools.partial(_mix_out_kernel, layer=layer, seg=seg, head_dim=head_dim, group_dim=group_dim),
        grid=(b, nc),
        in_specs=[
            pl.BlockSpec(memory_space=pltpu.SMEM),
            pl.BlockSpec((None, c, n), lambda i, t: (i, t, 0)),
            state_spec,
            state_spec,
            pl.BlockSpec((None, c, d), lambda i, t: (i, t, 0)),
            pl.BlockSpec((None, SG_GROUPS, SG_CHUNK, SG_CHUNK), lambda i, t: (layer, 0, 0, 0)),
            pl.BlockSpec((None, SG_CHUNK, seg), lambda i, t: (layer, 0, 0)),
            pl.BlockSpec((None, d, d), lambda i, t: (layer, 0, 0)),
        ],
        out_specs=pl.BlockSpec((None, c, d), lambda i, t: (i, t, 0)),
        out_shape=jax.ShapeDtypeStruct((b, s, d), F32),
        scratch_shapes=[pltpu.VMEM((c, d), BF16)],
        compiler_params=pltpu.CompilerParams(
            dimension_semantics=("arbitrary", "arbitrary"), vmem_limit_bytes=_vmem_limit(est)),
        name="mix_out",
    )(lg, proj3, sf, sb, h3, sg_w, sg_bias, w_out)


def _ffn_kernel(h_ref, g_ref, wg_ref, wu_ref, wd_ref, o_ref, hn_ref, *, row_chunk):
    def swiglu_down(hn):
        gate = jnp.dot(hn, wg_ref[...], preferred_element_type=F32)
        up = jnp.dot(hn, wu_ref[...], preferred_element_type=F32)
        act = (gate * _sigmoid(gate) * up).astype(BF16)
        return jnp.dot(act, wd_ref[...], preferred_element_type=F32)

    @pl.when(pl.program_id(1) == 0)
    def _():
        for r in range(h_ref.shape[0] // row_chunk):
            rows = pl.ds(r * row_chunk, row_chunk)
            x = h_ref[rows, :]
            hn = (_rms(x) * g_ref[...]).astype(BF16)
            hn_ref[rows, :] = hn
            o_ref[rows, :] = x + swiglu_down(hn)

    @pl.when(pl.program_id(1) > 0)
    def _():
        o_ref[...] += swiglu_down(hn_ref[...])


def _ffn(h, gain, w_gate, w_up, w_down, *, layer, tm, tf):
    t, d = h.shape
    dff = w_gate.shape[2]
    est = 4 * tm * d * 4 + tm * d * 2 + 2 * 3 * d * tf * 2 + 3 * tm * tf * 4
    return pl.pallas_call(
        functools.partial(_ffn_kernel, row_chunk=_tile(tm, ROW_CHUNK)),
        grid=(t // tm, dff // tf),
        in_specs=[
            pl.BlockSpec((tm, d), lambda i, j: (i, 0)),
            pl.BlockSpec((None, 1, d), lambda i, j: (layer, 0, 0)),
            pl.BlockSpec((None, d, tf), lambda i, j: (layer, 0, j)),
            pl.BlockSpec((None, d, tf), lambda i, j: (layer, 0, j)),
            pl.BlockSpec((None, tf, d), lambda i, j: (layer, j, 0)),
        ],
        out_specs=pl.BlockSpec((tm, d), lambda i, j: (i, 0)),
        out_shape=jax.ShapeDtypeStruct((t, d), F32),
        scratch_shapes=[pltpu.VMEM((tm, d), BF16)],
        compiler_params=pltpu.CompilerParams(
            dimension_semantics=("arbitrary", "arbitrary"), vmem_limit_bytes=_vmem_limit(est)),
        name="ffn",
    )(h, gain, w_gate, w_up, w_down)


def _ple_kernel(h_ref, p_ref, g_ref, wg_ref, wp_ref, gf_ref, o_ref, *, final, row_chunk):
    for r in range(h_ref.shape[0] // row_chunk):
        rows = pl.ds(r * row_chunk, row_chunk)
        x = h_ref[rows, :]
        hn = (_rms(x) * g_ref[...]).astype(BF16)
        gate = _sigmoid(jnp.dot(hn, wg_ref[...], preferred_element_type=F32))
        emb = jnp.dot(p_ref[rows, :].astype(BF16), wp_ref[...], preferred_element_type=F32)
        y = x + emb * gate
        if final:
            y = _rms(y) * gf_ref[...]
        o_ref[rows, :] = y


def _ple(h, p, gain, w_gate, w_proj, gain_final, *, layer, tm, final):
    t, d = h.shape
    pd = p.shape[2]
    est = 4 * tm * d * 4 + 2 * tm * pd * 4 + 2 * d * d * 2 + 2 * pd * d * 2 + 4 * tm * d * 4
    return pl.pallas_call(
        functools.partial(_ple_kernel, final=final, row_chunk=_tile(tm, ROW_CHUNK)),
        grid=(t // tm,),
        in_specs=[
            pl.BlockSpec((tm, d), lambda i: (i, 0)),
            pl.BlockSpec((None, tm, pd), lambda i: (layer, i, 0)),
            pl.BlockSpec((None, 1, d), lambda i: (layer, 0, 0)),
            pl.BlockSpec((None, d, d), lambda i: (layer, 0, 0)),
            pl.BlockSpec((None, pd, d), lambda i: (layer, 0, 0)),
            pl.BlockSpec((1, d), lambda i: (0, 0)),
        ],
        out_specs=pl.BlockSpec((tm, d), lambda i: (i, 0)),
        out_shape=jax.ShapeDtypeStruct((t, d), F32),
        compiler_params=pltpu.CompilerParams(
            dimension_semantics=("arbitrary",), vmem_limit_bytes=_vmem_limit(est)),
        name="ple",
    )(h, p, gain, w_gate, w_proj, gain_final)


def _trunk(x, p, params, log_gamma, cos, sin):
    b, s, d = x.shape
    depth = p.shape[0]
    t = b * s
    seg = params["w_in"].shape[2] // 6
    assert s % RET_TILE == 0 and RET_TILE % SG_CHUNK == 0

    tm_proj = _tile(s, 1024)
    tm_ffn = _tile(t, 1024)
    tf = _tile(params["w_ffn_gate"].shape[2], 256)
    tm_ple = _tile(t, 512)

    h = x.reshape(t, d)
    p = p.reshape(depth, t, -1)
    for l in range(depth):
        proj = _in_proj(h, params["norm_mix_g"], params["w_in"], cos, sin,
                        params["sg_ln_g"], params["sg_ln_b"], layer=l, seq=s, tm=tm_proj)
        proj3 = proj.reshape(b, s, 6 * seg)
        sf, sb = _ret_state(proj3, log_gamma, layer=l, seg=seg)
        h3 = _mix_out(proj3, sf, sb, h.reshape(b, s, d), log_gamma, params["sg_w"],
                      params["sg_bias"], params["w_out"], layer=l, seg=seg)
        h = _ffn(h3.reshape(t, d), params["norm_ffn_g"], params["w_ffn_gate"],
                 params["w_ffn_up"], params["w_ffn_down"], layer=l, tm=tm_ffn, tf=tf)
        h = _ple(h, p, params["norm_ple_g"], params["w_ple_gate"], params["w_ple_proj"],
                 params["norm_final_g"], layer=l, tm=tm_ple, final=(l == depth - 1))
    return h.reshape(b, s, d)


def kernel(x_prompt, x_sample, p_prompt, p_sample, norm_mix_g, w_in, ret_decay, sg_ln_g, sg_ln_b,
           sg_w, sg_b, w_out, norm_ffn_g, w_ffn_gate, w_ffn_up, w_ffn_down, norm_ple_g, w_ple_gate,
           w_ple_proj, norm_final_g):
    depth, d = norm_mix_g.shape
    seg = w_in.shape[2] // 6
    group_dim = seg // SG_GROUPS
    half = seg // RET_HEADS // 2
    params = {
        "norm_mix_g": norm_mix_g.reshape(depth, 1, d),
        "w_in": w_in.astype(BF16),
        "sg_ln_g": sg_ln_g.reshape(depth, 1, seg),
        "sg_ln_b": sg_ln_b.reshape(depth, 1, seg),
        "sg_w": sg_w.astype(BF16),
        "sg_bias": jnp.repeat(jnp.swapaxes(sg_b, 1, 2), group_dim, axis=2),
        "w_out": w_out.astype(BF16),
        "norm_ffn_g": norm_ffn_g.reshape(depth, 1, d),
        "w_ffn_gate": w_ffn_gate.astype(BF16),
        "w_ffn_up": w_ffn_up.astype(BF16),
        "w_ffn_down": w_ffn_down.astype(BF16),
        "norm_ple_g": norm_ple_g.reshape(depth, 1, d),
        "w_ple_gate": w_ple_gate.astype(BF16),
        "w_ple_proj": w_ple_proj.astype(BF16),
        "norm_final_g": norm_final_g.reshape(1, d),
    }
    log_gamma = jnp.log1p(-jnp.exp2(-5.0 - ret_decay.astype(F32)))
    max_seq = max(x_prompt.shape[1], x_sample.shape[1])
    inv_freq = ROPE_BASE ** (-jnp.arange(half, dtype=F32) / half)
    ang = jnp.arange(max_seq, dtype=F32)[:, None] * inv_freq[None, :]
    cos = jnp.cos(ang)
    sin = jnp.sin(ang)
    y_prompt = _trunk(x_prompt, p_prompt, params, log_gamma, cos, sin)
    y_sample = _trunk(x_sample, p_sample, params, log_gamma, cos, sin)
    return (y_prompt, y_sample)
```

```python
import functools

import jax
import jax.numpy as jnp
from jax import lax
from jax.experimental import pallas as pl
from jax.experimental.pallas import tpu as pltpu

RET_HEADS = 4
SG_GROUPS = 8
SG_CHUNK = 128
ROPE_BASE = 10000.0
EPS = 1e-6

RET_TILE = 256
ROW_CHUNK = 256

F32 = jnp.float32
BF16 = jnp.bfloat16
V7X_VMEM_BYTES = 64 * 1024 * 1024


def _vmem_limit(estimate_bytes):
    return int(min(V7X_VMEM_BYTES - 6 * 1024 * 1024, estimate_bytes * 1.25 + 4 * 1024 * 1024))


def _tile(total, preferred):
    tile = min(total, preferred)
    assert total % tile == 0, (total, tile)
    return tile


def _rms(x):
    return x * lax.rsqrt(jnp.mean(x * x, axis=-1, keepdims=True) + EPS)


def _sigmoid(x):
    return 1.0 / (1.0 + jnp.exp(-x))


def _in_proj_kernel(h_ref, g_ref, w_ref, cos_ref, sin_ref, lng_ref, lnb_ref, o_ref,
                    *, head_dim, group_dim, row_chunk):
    tm = o_ref.shape[0]
    width = o_ref.shape[1] // 6
    half = head_dim // 2

    def rotary(acc, rows, base, scale):
        c = cos_ref[rows, :]
        s = sin_ref[rows, :]
        for hd in range(width // head_dim):
            lo = hd * head_dim
            x1 = acc[:, lo:lo + half]
            x2 = acc[:, lo + half:lo + head_dim]
            o_ref[rows, base + lo:base + lo + half] = ((x1 * c - x2 * s) * scale).astype(BF16)
            o_ref[rows, base + lo + half:base + lo + head_dim] = ((x1 * s + x2 * c) * scale).astype(BF16)

    def group_norm(acc, rows, base):
        for g in range(width // group_dim):
            lo = g * group_dim
            x = jax.nn.gelu(acc[:, lo:lo + group_dim])
            xc = x - jnp.mean(x, axis=-1, keepdims=True)
            y = xc * lax.rsqrt(jnp.mean(xc * xc, axis=-1, keepdims=True) + EPS)
            o_ref[rows, base + lo:base + lo + group_dim] = (
                y * lng_ref[:, lo:lo + group_dim] + lnb_ref[:, lo:lo + group_dim]).astype(BF16)

    def plain(fn):
        def epilogue(acc, rows, base):
            o_ref[rows, base:base + width] = fn(acc).astype(BF16)
        return epilogue

    epilogues = (
        lambda acc, rows, base: rotary(acc, rows, base, 1.0),
        lambda acc, rows, base: rotary(acc, rows, base, head_dim ** -0.5),
        plain(lambda acc: acc),
        plain(lambda acc: acc * _sigmoid(acc)),
        plain(jax.nn.gelu),
        group_norm,
    )

    for r in range(tm // row_chunk):
        rows = pl.ds(r * row_chunk, row_chunk)
        hn = (_rms(h_ref[rows, :]) * g_ref[...]).astype(BF16)
        for seg_index, epilogue in enumerate(epilogues):
            base = seg_index * width
            acc = jnp.dot(hn, w_ref[:, base:base + width], preferred_element_type=F32)
            epilogue(acc, rows, base)


def _in_proj(h, gain, w, cos, sin, ln_g, ln_b, *, layer, seq, tm):
    t, d = h.shape
    n = w.shape[2]
    seg = n // 6
    head_dim = seg // RET_HEADS
    group_dim = seg // SG_GROUPS
    pos_blocks = seq // tm
    row_chunk = _tile(tm, ROW_CHUNK)
    est = (2 * tm * d * 4 + d * n * 2 + 2 * tm * n * 2 + row_chunk * d * 2
           + 4 * tm * (head_dim // 2) * 4 + 6 * row_chunk * seg * 4)
    return pl.pallas_call(
        functools.partial(_in_proj_kernel, head_dim=head_dim, group_dim=group_dim, row_chunk=row_chunk),
        grid=(t // tm,),
        in_specs=[
            pl.BlockSpec((tm, d), lambda i: (i, 0)),
            pl.BlockSpec((None, 1, d), lambda i: (layer, 0, 0)),
            pl.BlockSpec((None, d, n), lambda i: (layer, 0, 0), pipeline_mode=pl.Buffered(1)),
            pl.BlockSpec((tm, head_dim // 2), lambda i: (i % pos_blocks, 0)),
            pl.BlockSpec((tm, head_dim // 2), lambda i: (i % pos_blocks, 0)),
            pl.BlockSpec((None, 1, seg), lambda i: (layer, 0, 0)),
            pl.BlockSpec((None, 1, seg), lambda i: (layer, 0, 0)),
        ],
        out_specs=pl.BlockSpec((tm, n), lambda i: (i, 0)),
        out_shape=jax.ShapeDtypeStruct((t, n), BF16),
        compiler_params=pltpu.CompilerParams(
            dimension_semantics=("arbitrary",), vmem_limit_bytes=_vmem_limit(est)),
        name="in_proj",
    )(h, gain, w, cos, sin, ln_g, ln_b)


def _ret_state_kernel(lg_ref, kf_ref, vf_ref, kb_ref, vb_ref, sf_ref, sb_ref, st_ref,
                      *, layer, head_dim):
    t = pl.program_id(1)
    c = kf_ref.shape[0]

    @pl.when(t == 0)
    def _():
        st_ref[...] = jnp.zeros_like(st_ref)

    pos = lax.broadcasted_iota(jnp.int32, (c, 1), 0).astype(F32)
    for hd in range(RET_HEADS):
        sl = slice(hd * head_dim, (hd + 1) * head_dim)
        for d, (k_ref, v_ref, s_ref) in enumerate(((kf_ref, vf_ref, sf_ref), (kb_ref, vb_ref, sb_ref))):
            lg = lg_ref[layer, d, hd]
            k_dec = jnp.exp(lg * ((c - 1.0 - pos) if d == 0 else pos))
            c_dec = jnp.exp(jnp.full((1, head_dim), lg * c, F32))
            s_old = st_ref[d, hd]
            s_ref[hd] = s_old.astype(BF16)
            kd = (k_ref[:, sl].astype(F32) * k_dec).astype(BF16)
            u = lax.dot_general(kd, v_ref[:, sl], (((0,), (0,)), ((), ())),
                                preferred_element_type=F32)
            st_ref[d, hd] = s_old * c_dec + u


def _ret_state(proj3, lg, *, layer, seg):
    b, s, _ = proj3.shape
    c = RET_TILE
    nc = s // c
    head_dim = seg // RET_HEADS
    state = jax.ShapeDtypeStruct((b, nc, RET_HEADS, head_dim, head_dim), BF16)
    state_spec_f = pl.BlockSpec((None, None, RET_HEADS, head_dim, head_dim), lambda i, t: (i, t, 0, 0, 0))
    state_spec_b = pl.BlockSpec((None, None, RET_HEADS, head_dim, head_dim),
                                lambda i, t: (i, nc - 1 - t, 0, 0, 0))
    est = 8 * c * seg * 2 + 4 * RET_HEADS * head_dim * head_dim * 2 + 2 * RET_HEADS * head_dim * head_dim * 4
    return pl.pallas_call(
        functools.partial(_ret_state_kernel, layer=layer, head_dim=head_dim),
        grid=(b, nc),
        in_specs=[
            pl.BlockSpec(memory_space=pltpu.SMEM),
            pl.BlockSpec((None, c, seg), lambda i, t: (i, t, 1)),
            pl.BlockSpec((None, c, seg), lambda i, t: (i, t, 2)),
            pl.BlockSpec((None, c, seg), lambda i, t: (i, nc - 1 - t, 1)),
            pl.BlockSpec((None, c, seg), lambda i, t: (i, nc - 1 - t, 2)),
        ],
        out_specs=[state_spec_f, state_spec_b],
        out_shape=[state, state],
        scratch_shapes=[pltpu.VMEM((2, RET_HEADS, head_dim, head_dim), F32)],
        compiler_params=pltpu.CompilerParams(
            dimension_semantics=("arbitrary", "arbitrary"), vmem_limit_bytes=_vmem_limit(est)),
        name="ret_state",
    )(lg, proj3, proj3, proj3, proj3)


def _mix_out_kernel(lg_ref, p_ref, sf_ref, sb_ref, h_ref, sgw_ref, sgb_ref, wo_ref, o_ref, y_ref,
                    *, layer, seg, head_dim, group_dim):
    c = p_ref.shape[0]
    row = lax.broadcasted_iota(jnp.int32, (c, c), 0)
    col = lax.broadcasted_iota(jnp.int32, (c, c), 1)
    dist = (row - col).astype(F32)
    causal = row >= col
    pos = lax.broadcasted_iota(jnp.int32, (c, 1), 0).astype(F32)

    for hd in range(RET_HEADS):
        lo = hd * head_dim
        lgf = lg_ref[layer, 0, hd]
        lgb = lg_ref[layer, 1, hd]
        q = p_ref[:, lo:lo + head_dim]
        k = p_ref[:, seg + lo:seg + lo + head_dim]
        v = p_ref[:, 2 * seg + lo:2 * seg + lo + head_dim]
        gate = p_ref[:, 3 * seg + lo:3 * seg + lo + head_dim]
        scores = lax.dot_general(q, k, (((1,), (1,)), ((), ())), preferred_element_type=F32)
        decay = jnp.exp(jnp.where(causal, lgf, -lgb) * dist)
        qf = q.astype(F32)
        lhs = jnp.concatenate([
            (scores * decay).astype(BF16),
            (qf * jnp.exp(lgf * (pos + 1.0))).astype(BF16),
            (qf * jnp.exp(lgb * (c - pos))).astype(BF16)], axis=1)
        rhs = jnp.concatenate([v, sf_ref[hd], sb_ref[hd]], axis=0)
        y = jnp.dot(lhs, rhs, preferred_element_type=F32)
        y_ref[:, lo:lo + head_dim] = (gate.astype(F32) * _rms(y)).astype(BF16)

    for g in range(SG_GROUPS):
        lo = g * group_dim
        u = p_ref[:, 4 * seg + lo:4 * seg + lo + group_dim]
        vn = p_ref[:, 5 * seg + lo:5 * seg + lo + group_dim]
        n_sub = c // SG_CHUNK
        vcat = jnp.concatenate([vn[n * SG_CHUNK:(n + 1) * SG_CHUNK] for n in range(n_sub)], axis=1)
        z = jnp.dot(sgw_ref[g], vcat, preferred_element_type=F32)
        bias = sgb_ref[:, lo:lo + group_dim]
        for n in range(n_sub):
            zn = z[:, n * group_dim:(n + 1) * group_dim] + bias
            un = u[n * SG_CHUNK:(n + 1) * SG_CHUNK].astype(F32)
            y_ref[n * SG_CHUNK:(n + 1) * SG_CHUNK, seg + lo:seg + lo + group_dim] = (un * zn).astype(BF16)

    o_ref[...] = h_ref[...] + jnp.dot(y_ref[...], wo_ref[...], preferred_element_type=F32)


def _mix_out(proj3, sf, sb, h3, lg, sg_w, sg_bias, w_out, *, layer, seg):
    b, s, n = proj3.shape
    d = h3.shape[-1]
    c = RET_TILE
    nc = s // c
    head_dim = seg // RET_HEADS
    group_dim = seg // SG_GROUPS
    state_spec = pl.BlockSpec((None, None, RET_HEADS, head_dim, head_dim), lambda i, t: (i, t, 0, 0, 0))
    est = (2 * c * n * 2 + 4 * RET_HEADS * head_dim * head_dim * 2 + 4 * c * d * 4
           + 2 * d * d * 2 + 2 * SG_CHUNK * seg * 4 + c * d * 2 + 8 * c * c * 4)
    return pl.pallas_call(
        functools.partial(_mix_out_kernel, layer=layer, seg=seg, head_dim=head_dim, group_dim=group_dim),
        grid=(b, nc),
        in_specs=[
            pl.BlockSpec(memory_space=pltpu.SMEM),
            pl.BlockSpec((None, c, n), lambda i, t: (i, t, 0)),
            state_spec,
            state_spec,
            pl.BlockSpec((None, c, d), lambda i, t: (i, t, 0)),
            pl.BlockSpec((None, SG_GROUPS, SG_CHUNK, SG_CHUNK), lambda i, t: (layer, 0, 0, 0)),
            pl.BlockSpec((None, SG_CHUNK, seg), lambda i, t: (layer, 0, 0)),
            pl.BlockSpec((None, d, d), lambda i, t: (layer, 0, 0)),
        ],
        out_specs=pl.BlockSpec((None, c, d), lambda i, t: (i, t, 0)),
        out_shape=jax.ShapeDtypeStruct((b, s, d), F32),
        scratch_shapes=[pltpu.VMEM((c, d), BF16)],
        compiler_params=pltpu.CompilerParams(
            dimension_semantics=("arbitrary", "arbitrary"), vmem_limit_bytes=_vmem_limit(est)),
        name="mix_out",
    )(lg, proj3, sf, sb, h3, sg_w, sg_bias, w_out)


def _ffn_kernel(h_ref, g_ref, wg_ref, wu_ref, wd_ref, o_ref, hn_ref, *, row_chunk):
    def swiglu_down(hn):
        gate = jnp.dot(hn, wg_ref[...], preferred_element_type=F32)
        up = jnp.dot(hn, wu_ref[...], preferred_element_type=F32)
        act = (gate * _sigmoid(gate) * up).astype(BF16)
        return jnp.dot(act, wd_ref[...], preferred_element_type=F32)

    @pl.when(pl.program_id(1) == 0)
    def _():
        for r in range(h_ref.shape[0] // row_chunk):
            rows = pl.ds(r * row_chunk, row_chunk)
            x = h_ref[rows, :]
            hn = (_rms(x) * g_ref[...]).astype(BF16)
            hn_ref[rows, :] = hn
            o_ref[rows, :] = x + swiglu_down(hn)

    @pl.when(pl.program_id(1) > 0)
    def _():
        o_ref[...] += swiglu_down(hn_ref[...])


def _ffn(h, gain, w_gate, w_up, w_down, *, layer, tm, tf):
    t, d = h.shape
    dff = w_gate.shape[2]
    est = 4 * tm * d * 4 + tm * d * 2 + 2 * 3 * d * tf * 2 + 3 * tm * tf * 4
    return pl.pallas_call(
        functools.partial(_ffn_kernel, row_chunk=_tile(tm, ROW_CHUNK)),
        grid=(t // tm, dff // tf),
        in_specs=[
            pl.BlockSpec((tm, d), lambda i, j: (i, 0)),
            pl.BlockSpec((None, 1, d), lambda i, j: (layer, 0, 0)),
            pl.BlockSpec((None, d, tf), lambda i, j: (layer, 0, j)),
            pl.BlockSpec((None, d, tf), lambda i, j: (layer, 0, j)),
            pl.BlockSpec((None, tf, d), lambda i, j: (layer, j, 0)),
        ],
        out_specs=pl.BlockSpec((tm, d), lambda i, j: (i, 0)),
        out_shape=jax.ShapeDtypeStruct((t, d), F32),
        scratch_shapes=[pltpu.VMEM((tm, d), BF16)],
        compiler_params=pltpu.CompilerParams(
            dimension_semantics=("arbitrary", "arbitrary"), vmem_limit_bytes=_vmem_limit(est)),
        name="ffn",
    )(h, gain, w_gate, w_up, w_down)


def _ple_kernel(h_ref, p_ref, g_ref, wg_ref, wp_ref, gf_ref, o_ref, *, final, row_chunk):
    for r in range(h_ref.shape[0] // row_chunk):
        rows = pl.ds(r * row_chunk, row_chunk)
        x = h_ref[rows, :]
        hn = (_rms(x) * g_ref[...]).astype(BF16)
        gate = _sigmoid(jnp.dot(hn, wg_ref[...], preferred_element_type=F32))
        emb = jnp.dot(p_ref[rows, :].astype(BF16), wp_ref[...], preferred_element_type=F32)
        y = x + emb * gate
        if final:
            y = _rms(y) * gf_ref[...]
        o_ref[rows, :] = y


def _ple(h, p, gain, w_gate, w_proj, gain_final, *, layer, tm, final):
    t, d = h.shape
    pd = p.shape[2]
    row_chunk = _tile(tm, ROW_CHUNK)
    est = 4 * tm * d * 4 + 2 * tm * pd * 4 + d * d * 2 + pd * d * 2 + 4 * row_chunk * d * 4
    return pl.pallas_call(
        functools.partial(_ple_kernel, final=final, row_chunk=row_chunk),
        grid=(t // tm,),
        in_specs=[
            pl.BlockSpec((tm, d), lambda i: (i, 0)),
            pl.BlockSpec((None, tm, pd), lambda i: (layer, i, 0)),
            pl.BlockSpec((None, 1, d), lambda i: (layer, 0, 0)),
            pl.BlockSpec((None, d, d), lambda i: (layer, 0, 0), pipeline_mode=pl.Buffered(1)),
            pl.BlockSpec((None, pd, d), lambda i: (layer, 0, 0), pipeline_mode=pl.Buffered(1)),
            pl.BlockSpec((1, d), lambda i: (0, 0)),
        ],
        out_specs=pl.BlockSpec((tm, d), lambda i: (i, 0)),
        out_shape=jax.ShapeDtypeStruct((t, d), F32),
        compiler_params=pltpu.CompilerParams(
            dimension_semantics=("arbitrary",), vmem_limit_bytes=_vmem_limit(est)),
        name="ple",
    )(h, p, gain, w_gate, w_proj, gain_final)


def _trunk(x, p, params, log_gamma, cos, sin):
    b, s, d = x.shape
    depth = p.shape[0]
    t = b * s
    seg = params["w_in"].shape[2] // 6
    assert s % RET_TILE == 0 and RET_TILE % SG_CHUNK == 0

    tm_proj = _tile(s, 512)
    tm_ffn = _tile(t, 1024)
    tf = _tile(params["w_ffn_gate"].shape[2], 512)
    tm_ple = _tile(t, 1024)

    h = x.reshape(t, d)
    p = p.reshape(depth, t, -1)
    for l in range(depth):
        proj = _in_proj(h, params["norm_mix_g"], params["w_in"], cos, sin,
                        params["sg_ln_g"], params["sg_ln_b"], layer=l, seq=s, tm=tm_proj)
        proj3 = proj.reshape(b, s, 6 * seg)
        sf, sb = _ret_state(proj3, log_gamma, layer=l, seg=seg)
        h3 = _mix_out(proj3, sf, sb, h.reshape(b, s, d), log_gamma, params["sg_w"],
                      params["sg_bias"], params["w_out"], layer=l, seg=seg)
        h = _ffn(h3.reshape(t, d), params["norm_ffn_g"], params["w_ffn_gate"],
                 params["w_ffn_up"], params["w_ffn_down"], layer=l, tm=tm_ffn, tf=tf)
        h = _ple(h, p, params["norm_ple_g"], params["w_ple_gate"], params["w_ple_proj"],
                 params["norm_final_g"], layer=l, tm=tm_ple, final=(l == depth - 1))
    return h.reshape(b, s, d)


def kernel(x_prompt, x_sample, p_prompt, p_sample, norm_mix_g, w_in, ret_decay, sg_ln_g, sg_ln_b,
           sg_w, sg_b, w_out, norm_ffn_g, w_ffn_gate, w_ffn_up, w_ffn_down, norm_ple_g, w_ple_gate,
           w_ple_proj, norm_final_g):
    depth, d = norm_mix_g.shape
    seg = w_in.shape[2] // 6
    group_dim = seg // SG_GROUPS
    half = seg // RET_HEADS // 2
    params = {
        "norm_mix_g": norm_mix_g.reshape(depth, 1, d),
        "w_in": w_in.astype(BF16),
        "sg_ln_g": sg_ln_g.reshape(depth, 1, seg),
        "sg_ln_b": sg_ln_b.reshape(depth, 1, seg),
        "sg_w": sg_w.astype(BF16),
        "sg_bias": jnp.repeat(jnp.swapaxes(sg_b, 1, 2), group_dim, axis=2),
        "w_out": w_out.astype(BF16),
        "norm_ffn_g": norm_ffn_g.reshape(depth, 1, d),
        "w_ffn_gate": w_ffn_gate.astype(BF16),
        "w_ffn_up": w_ffn_up.astype(BF16),
        "w_ffn_down": w_ffn_down.astype(BF16),
        "norm_ple_g": norm_ple_g.reshape(depth, 1, d),
        "w_ple_gate": w_ple_gate.astype(BF16),
        "w_ple_proj": w_ple_proj.astype(BF16),
        "norm_final_g": norm_final_g.reshape(1, d),
    }
    log_gamma = jnp.log1p(-jnp.exp2(-5.0 - ret_decay.astype(F32)))
    max_seq = max(x_prompt.shape[1], x_sample.shape[1])
    inv_freq = ROPE_BASE ** (-jnp.arange(half, dtype=F32) / half)
    ang = jnp.arange(max_seq, dtype=F32)[:, None] * inv_freq[None, :]
    cos = jnp.cos(ang)
    sin = jnp.sin(ang)
    y_prompt = _trunk(x_prompt, p_prompt, params, log_gamma, cos, sin)
    y_sample = _trunk(x_sample, p_sample, params, log_gamma, cos, sin)
    return (y_prompt, y_sample)
```

```python
import functools

import jax
import jax.numpy as jnp
from jax import lax
from jax.experimental import pallas as pl
from jax.experimental.pallas import tpu as pltpu

RET_HEADS = 4
SG_GROUPS = 8
SG_CHUNK = 128
ROPE_BASE = 10000.0
EPS = 1e-6

RET_TILE = 256
ROW_CHUNK = 256

F32 = jnp.float32
BF16 = jnp.bfloat16
V7X_VMEM_BYTES = 64 * 1024 * 1024


def _vmem_limit(estimate_bytes):
    return int(min(V7X_VMEM_BYTES - 6 * 1024 * 1024, estimate_bytes * 1.25 + 4 * 1024 * 1024))


def _tile(total, preferred):
    tile = min(total, preferred)
    assert total % tile == 0, (total, tile)
    return tile


def _rms(x):
    return x * lax.rsqrt(jnp.mean(x * x, axis=-1, keepdims=True) + EPS)


def _sigmoid(x):
    return 1.0 / (1.0 + jnp.exp(-x))


def _in_proj_kernel(h_ref, g_ref, w_ref, cos_ref, sin_ref, lng_ref, lnb_ref, o_ref,
                    *, head_dim, group_dim, row_chunk):
    tm = o_ref.shape[0]
    width = o_ref.shape[1] // 6
    half = head_dim // 2

    def rotary(acc, rows, base, scale):
        c = cos_ref[rows, :]
        s = sin_ref[rows, :]
        for hd in range(width // head_dim):
            lo = hd * head_dim
            x1 = acc[:, lo:lo + half]
            x2 = acc[:, lo + half:lo + head_dim]
            o_ref[rows, base + lo:base + lo + half] = ((x1 * c - x2 * s) * scale).astype(BF16)
            o_ref[rows, base + lo + half:base + lo + head_dim] = ((x1 * s + x2 * c) * scale).astype(BF16)

    def group_norm(acc, rows, base):
        for g in range(width // group_dim):
            lo = g * group_dim
            x = jax.nn.gelu(acc[:, lo:lo + group_dim])
            xc = x - jnp.mean(x, axis=-1, keepdims=True)
            y = xc * lax.rsqrt(jnp.mean(xc * xc, axis=-1, keepdims=True) + EPS)
            o_ref[rows, base + lo:base + lo + group_dim] = (
                y * lng_ref[:, lo:lo + group_dim] + lnb_ref[:, lo:lo + group_dim]).astype(BF16)

    def plain(fn):
        def epilogue(acc, rows, base):
            o_ref[rows, base:base + width] = fn(acc).astype(BF16)
        return epilogue

    epilogues = (
        lambda acc, rows, base: rotary(acc, rows, base, 1.0),
        lambda acc, rows, base: rotary(acc, rows, base, head_dim ** -0.5),
        plain(lambda acc: acc),
        plain(lambda acc: acc * _sigmoid(acc)),
        plain(jax.nn.gelu),
        group_norm,
    )

    for r in range(tm // row_chunk):
        rows = pl.ds(r * row_chunk, row_chunk)
        hn = (_rms(h_ref[rows, :]) * g_ref[...]).astype(BF16)
        for seg_index, epilogue in enumerate(epilogues):
            base = seg_index * width
            acc = jnp.dot(hn, w_ref[:, base:base + width], preferred_element_type=F32)
            epilogue(acc, rows, base)


def _in_proj(h, gain, w, cos, sin, ln_g, ln_b, *, layer, seq, tm):
    t, d = h.shape
    n = w.shape[2]
    seg = n // 6
    head_dim = seg // RET_HEADS
    group_dim = seg // SG_GROUPS
    pos_blocks = seq // tm
    row_chunk = _tile(tm, ROW_CHUNK)
    est = (2 * tm * d * 4 + d * n * 2 + 2 * tm * n * 2 + row_chunk * d * 2
           + 4 * tm * (head_dim // 2) * 4 + 6 * row_chunk * seg * 4)
    return pl.pallas_call(
        functools.partial(_in_proj_kernel, head_dim=head_dim, group_dim=group_dim, row_chunk=row_chunk),
        grid=(t // tm,),
        in_specs=[
            pl.BlockSpec((tm, d), lambda i: (i, 0)),
            pl.BlockSpec((None, 1, d), lambda i: (layer, 0, 0)),
            pl.BlockSpec((None, d, n), lambda i: (layer, 0, 0), pipeline_mode=pl.Buffered(1)),
            pl.BlockSpec((tm, head_dim // 2), lambda i: (i % pos_blocks, 0)),
            pl.BlockSpec((tm, head_dim // 2), lambda i: (i % pos_blocks, 0)),
            pl.BlockSpec((None, 1, seg), lambda i: (layer, 0, 0)),
            pl.BlockSpec((None, 1, seg), lambda i: (layer, 0, 0)),
        ],
        out_specs=pl.BlockSpec((tm, n), lambda i: (i, 0)),
        out_shape=jax.ShapeDtypeStruct((t, n), BF16),
        compiler_params=pltpu.CompilerParams(
            dimension_semantics=("arbitrary",), vmem_limit_bytes=_vmem_limit(est)),
        name="in_proj",
    )(h, gain, w, cos, sin, ln_g, ln_b)


def _state_update(s_old, k, v, k_dec, c_dec):
    kd = (k.astype(F32) * k_dec).astype(BF16)
    u = lax.dot_general(kd, v, (((0,), (0,)), ((), ())), preferred_element_type=F32)
    return s_old * c_dec + u


def _ret_state_kernel(lg_ref, k_ref, v_ref, sb_ref, st_ref, *, layer, head_dim, chunk):
    @pl.when(pl.program_id(1) == 0)
    def _():
        st_ref[...] = jnp.zeros_like(st_ref)

    pos = lax.broadcasted_iota(jnp.int32, (chunk, 1), 0).astype(F32)
    for c in reversed(range(k_ref.shape[0] // chunk)):
        rows = pl.ds(c * chunk, chunk)
        for hd in range(RET_HEADS):
            cols = pl.ds(hd * head_dim, head_dim)
            lgb = lg_ref[layer, 1, hd]
            s_old = st_ref[hd]
            sb_ref[c, hd] = s_old.astype(BF16)
            st_ref[hd] = _state_update(
                s_old, k_ref[rows, cols], v_ref[rows, cols], jnp.exp(lgb * pos),
                jnp.exp(jnp.full((1, head_dim), lgb * chunk, F32)))


def _ret_state(proj3, lg, *, layer, seg, tile):
    b, s, _ = proj3.shape
    c = RET_TILE
    per_tile = tile // c
    nt = s // tile
    head_dim = seg // RET_HEADS
    est = 4 * tile * seg * 2 + 2 * per_tile * RET_HEADS * head_dim * head_dim * 2 + RET_HEADS * head_dim * head_dim * 4
    return pl.pallas_call(
        functools.partial(_ret_state_kernel, layer=layer, head_dim=head_dim, chunk=c),
        grid=(b, nt),
        in_specs=[
            pl.BlockSpec(memory_space=pltpu.SMEM),
            pl.BlockSpec((None, tile, seg), lambda i, t: (i, nt - 1 - t, 1)),
            pl.BlockSpec((None, tile, seg), lambda i, t: (i, nt - 1 - t, 2)),
        ],
        out_specs=pl.BlockSpec((None, per_tile, RET_HEADS, head_dim, head_dim),
                               lambda i, t: (i, nt - 1 - t, 0, 0, 0)),
        out_shape=jax.ShapeDtypeStruct((b, s // c, RET_HEADS, head_dim, head_dim), BF16),
        scratch_shapes=[pltpu.VMEM((RET_HEADS, head_dim, head_dim), F32)],
        compiler_params=pltpu.CompilerParams(
            dimension_semantics=("arbitrary", "arbitrary"), vmem_limit_bytes=_vmem_limit(est)),
        name="ret_state",
    )(lg, proj3, proj3)


def _mix_out_kernel(lg_ref, p_ref, sb_ref, h_ref, sgw_ref, sgb_ref, wo_ref, o_ref, y_ref, sf_ref,
                    *, layer, seg, head_dim, group_dim, chunk):
    tile = p_ref.shape[0]

    @pl.when(pl.program_id(1) == 0)
    def _():
        sf_ref[...] = jnp.zeros_like(sf_ref)

    row = lax.broadcasted_iota(jnp.int32, (chunk, chunk), 0)
    col = lax.broadcasted_iota(jnp.int32, (chunk, chunk), 1)
    dist = (row - col).astype(F32)
    causal = row >= col
    pos = lax.broadcasted_iota(jnp.int32, (chunk, 1), 0).astype(F32)

    for c in range(tile // chunk):
        rows = pl.ds(c * chunk, chunk)
        for hd in range(RET_HEADS):
            lo = hd * head_dim
            lgf = lg_ref[layer, 0, hd]
            lgb = lg_ref[layer, 1, hd]
            q = p_ref[rows, pl.ds(lo, head_dim)]
            k = p_ref[rows, pl.ds(seg + lo, head_dim)]
            v = p_ref[rows, pl.ds(2 * seg + lo, head_dim)]
            gate = p_ref[rows, pl.ds(3 * seg + lo, head_dim)]
            s_fwd = sf_ref[hd]
            scores = lax.dot_general(q, k, (((1,), (1,)), ((), ())), preferred_element_type=F32)
            decay = jnp.exp(jnp.where(causal, lgf, -lgb) * dist)
            qf = q.astype(F32)
            lhs = jnp.concatenate([
                (scores * decay).astype(BF16),
                (qf * jnp.exp(lgf * (pos + 1.0))).astype(BF16),
                (qf * jnp.exp(lgb * (chunk - pos))).astype(BF16)], axis=1)
            rhs = jnp.concatenate([v, s_fwd.astype(BF16), sb_ref[c, hd]], axis=0)
            y = jnp.dot(lhs, rhs, preferred_element_type=F32)
            y_ref[rows, pl.ds(lo, head_dim)] = (gate.astype(F32) * _rms(y)).astype(BF16)
            sf_ref[hd] = _state_update(
                s_fwd, k, v, jnp.exp(lgf * (chunk - 1.0 - pos)),
                jnp.exp(jnp.full((1, head_dim), lgf * chunk, F32)))

    n_sub = tile // SG_CHUNK
    for g in range(SG_GROUPS):
        lo = g * group_dim
        vcat = jnp.concatenate(
            [p_ref[pl.ds(n * SG_CHUNK, SG_CHUNK), pl.ds(5 * seg + lo, group_dim)] for n in range(n_sub)],
            axis=1)
        z = jnp.dot(sgw_ref[g], vcat, preferred_element_type=F32)
        bias = sgb_ref[:, lo:lo + group_dim]
        for n in range(n_sub):
            sub = pl.ds(n * SG_CHUNK, SG_CHUNK)
            zn = z[:, n * group_dim:(n + 1) * group_dim] + bias
            un = p_ref[sub, pl.ds(4 * seg + lo, group_dim)].astype(F32)
            y_ref[sub, pl.ds(seg + lo, group_dim)] = (un * zn).astype(BF16)

    o_ref[...] = h_ref[...] + jnp.dot(y_ref[...], wo_ref[...], preferred_element_type=F32)


def _mix_out(proj3, sb, h3, lg, sg_w, sg_bias, w_out, *, layer, seg, tile):
    b, s, n = proj3.shape
    d = h3.shape[-1]
    c = RET_TILE
    per_tile = tile // c
    head_dim = seg // RET_HEADS
    group_dim = seg // SG_GROUPS
    state_bytes = RET_HEADS * head_dim * head_dim
    est = (2 * tile * n * 2 + 2 * per_tile * state_bytes * 2 + 4 * tile * d * 4 + d * d * 2
           + 2 * SG_CHUNK * seg * 4 + tile * d * 2 + state_bytes * 4 + 8 * c * c * 4)
    return pl.pallas_call(
        functools.partial(_mix_out_kernel, layer=layer, seg=seg, head_dim=head_dim,
                          group_dim=group_dim, chunk=c),
        grid=(b, s // tile),
        in_specs=[
            pl.BlockSpec(memory_space=pltpu.SMEM),
            pl.BlockSpec((None, tile, n), lambda i, t: (i, t, 0)),
            pl.BlockSpec((None, per_tile, RET_HEADS, head_dim, head_dim), lambda i, t: (i, t, 0, 0, 0)),
            pl.BlockSpec((None, tile, d), lambda i, t: (i, t, 0)),
            pl.BlockSpec((None, SG_GROUPS, SG_CHUNK, SG_CHUNK), lambda i, t: (layer, 0, 0, 0)),
            pl.BlockSpec((None, SG_CHUNK, seg), lambda i, t: (layer, 0, 0)),
            pl.BlockSpec((None, d, d), lambda i, t: (layer, 0, 0), pipeline_mode=pl.Buffered(1)),
        ],
        out_specs=pl.BlockSpec((None, tile, d), lambda i, t: (i, t, 0)),
        out_shape=jax.ShapeDtypeStruct((b, s, d), F32),
        scratch_shapes=[pltpu.VMEM((tile, d), BF16),
                        pltpu.VMEM((RET_HEADS, head_dim, head_dim), F32)],
        compiler_params=pltpu.CompilerParams(
            dimension_semantics=("arbitrary", "arbitrary"), vmem_limit_bytes=_vmem_limit(est)),
        name="mix_out",
    )(lg, proj3, sb, h3, sg_w, sg_bias, w_out)


def _ffn_kernel(h_ref, g_ref, wg_ref, wu_ref, wd_ref, o_ref, hn_ref, *, row_chunk):
    def swiglu_down(hn):
        gate = jnp.dot(hn, wg_ref[...], preferred_element_type=F32)
        up = jnp.dot(hn, wu_ref[...], preferred_element_type=F32)
        act = (gate * _sigmoid(gate) * up).astype(BF16)
        return jnp.dot(act, wd_ref[...], preferred_element_type=F32)

    @pl.when(pl.program_id(1) == 0)
    def _():
        for r in range(h_ref.shape[0] // row_chunk):
            rows = pl.ds(r * row_chunk, row_chunk)
            x = h_ref[rows, :]
            hn = (_rms(x) * g_ref[...]).astype(BF16)
            hn_ref[rows, :] = hn
            o_ref[rows, :] = x + swiglu_down(hn)

    @pl.when(pl.program_id(1) > 0)
    def _():
        o_ref[...] += swiglu_down(hn_ref[...])


def _ffn(h, gain, w_gate, w_up, w_down, *, layer, tm, tf):
    t, d = h.shape
    dff = w_gate.shape[2]
    est = 4 * tm * d * 4 + tm * d * 2 + 2 * 3 * d * tf * 2 + 3 * tm * tf * 4
    return pl.pallas_call(
        functools.partial(_ffn_kernel, row_chunk=_tile(tm, ROW_CHUNK)),
        grid=(t // tm, dff // tf),
        in_specs=[
            pl.BlockSpec((tm, d), lambda i, j: (i, 0)),
            pl.BlockSpec((None, 1, d), lambda i, j: (layer, 0, 0)),
            pl.BlockSpec((None, d, tf), lambda i, j: (layer, 0, j)),
            pl.BlockSpec((None, d, tf), lambda i, j: (layer, 0, j)),
            pl.BlockSpec((None, tf, d), lambda i, j: (layer, j, 0)),
        ],
        out_specs=pl.BlockSpec((tm, d), lambda i, j: (i, 0)),
        out_shape=jax.ShapeDtypeStruct((t, d), F32),
        scratch_shapes=[pltpu.VMEM((tm, d), BF16)],
        compiler_params=pltpu.CompilerParams(
            dimension_semantics=("arbitrary", "arbitrary"), vmem_limit_bytes=_vmem_limit(est)),
        name="ffn",
    )(h, gain, w_gate, w_up, w_down)


def _ple_kernel(h_ref, p_ref, g_ref, wg_ref, wp_ref, gf_ref, o_ref, *, final, row_chunk):
    for r in range(h_ref.shape[0] // row_chunk):
        rows = pl.ds(r * row_chunk, row_chunk)
        x = h_ref[rows, :]
        hn = (_rms(x) * g_ref[...]).astype(BF16)
        gate = _sigmoid(jnp.dot(hn, wg_ref[...], preferred_element_type=F32))
        emb = jnp.dot(p_ref[rows, :].astype(BF16), wp_ref[...], preferred_element_type=F32)
        y = x + emb * gate
        if final:
            y = _rms(y) * gf_ref[...]
        o_ref[rows, :] = y


def _ple(h, p, gain, w_gate, w_proj, gain_final, *, layer, tm, final):
    t, d = h.shape
    pd = p.shape[2]
    row_chunk = _tile(tm, ROW_CHUNK)
    est = 4 * tm * d * 4 + 2 * tm * pd * 4 + d * d * 2 + pd * d * 2 + 4 * row_chunk * d * 4
    return pl.pallas_call(
        functools.partial(_ple_kernel, final=final, row_chunk=row_chunk),
        grid=(t // tm,),
        in_specs=[
            pl.BlockSpec((tm, d), lambda i: (i, 0)),
            pl.BlockSpec((None, tm, pd), lambda i: (layer, i, 0)),
            pl.BlockSpec((None, 1, d), lambda i: (layer, 0, 0)),
            pl.BlockSpec((None, d, d), lambda i: (layer, 0, 0), pipeline_mode=pl.Buffered(1)),
            pl.BlockSpec((None, pd, d), lambda i: (layer, 0, 0), pipeline_mode=pl.Buffered(1)),
            pl.BlockSpec((1, d), lambda i: (0, 0)),
        ],
        out_specs=pl.BlockSpec((tm, d), lambda i: (i, 0)),
        out_shape=jax.ShapeDtypeStruct((t, d), F32),
        compiler_params=pltpu.CompilerParams(
            dimension_semantics=("arbitrary",), vmem_limit_bytes=_vmem_limit(est)),
        name="ple",
    )(h, p, gain, w_gate, w_proj, gain_final)


def _trunk(x, p, params, log_gamma, cos, sin):
    b, s, d = x.shape
    depth = p.shape[0]
    t = b * s
    seg = params["w_in"].shape[2] // 6
    assert s % RET_TILE == 0 and RET_TILE % SG_CHUNK == 0

    mix_tile = _tile(s, 2 * RET_TILE)
    assert mix_tile % RET_TILE == 0
    tm_proj = _tile(s, 512)
    tm_ffn = _tile(t, 1024)
    tf = _tile(params["w_ffn_gate"].shape[2], 512)
    tm_ple = _tile(t, 1024)

    h = x.reshape(t, d)
    p = p.reshape(depth, t, -1)
    for l in range(depth):
        proj = _in_proj(h, params["norm_mix_g"], params["w_in"], cos, sin,
                        params["sg_ln_g"], params["sg_ln_b"], layer=l, seq=s, tm=tm_proj)
        proj3 = proj.reshape(b, s, 6 * seg)
        sb = _ret_state(proj3, log_gamma, layer=l, seg=seg, tile=mix_tile)
        h3 = _mix_out(proj3, sb, h.reshape(b, s, d), log_gamma, params["sg_w"],
                      params["sg_bias"], params["w_out"], layer=l, seg=seg, tile=mix_tile)
        h = _ffn(h3.reshape(t, d), params["norm_ffn_g"], params["w_ffn_gate"],
                 params["w_ffn_up"], params["w_ffn_down"], layer=l, tm=tm_ffn, tf=tf)
        h = _ple(h, p, params["norm_ple_g"], params["w_ple_gate"], params["w_ple_proj"],
                 params["norm_final_g"], layer=l, tm=tm_ple, final=(l == depth - 1))
    return h.reshape(b, s, d)


def kernel(x_prompt, x_sample, p_prompt, p_sample, norm_mix_g, w_in, ret_decay, sg_ln_g, sg_ln_b,
           sg_w, sg_b, w_out, norm_ffn_g, w_ffn_gate, w_ffn_up, w_ffn_down, norm_ple_g, w_ple_gate,
           w_ple_proj, norm_final_g):
    depth, d = norm_mix_g.shape
    seg = w_in.shape[2] // 6
    group_dim = seg // SG_GROUPS
    half = seg // RET_HEADS // 2
    params = {
        "norm_mix_g": norm_mix_g.reshape(depth, 1, d),
        "w_in": w_in.astype(BF16),
        "sg_ln_g": sg_ln_g.reshape(depth, 1, seg),
        "sg_ln_b": sg_ln_b.reshape(depth, 1, seg),
        "sg_w": sg_w.astype(BF16),
        "sg_bias": jnp.repeat(jnp.swapaxes(sg_b, 1, 2), group_dim, axis=2),
        "w_out": w_out.astype(BF16),
        "norm_ffn_g": norm_ffn_g.reshape(depth, 1, d),
        "w_ffn_gate": w_ffn_gate.astype(BF16),
        "w_ffn_up": w_ffn_up.astype(BF16),
        "w_ffn_down": w_ffn_down.astype(BF16),
        "norm_ple_g": norm_ple_g.reshape(depth, 1, d),
        "w_ple_gate": w_ple_gate.astype(BF16),
        "w_ple_proj": w_ple_proj.astype(BF16),
        "norm_final_g": norm_final_g.reshape(1, d),
    }
    log_gamma = jnp.log1p(-jnp.exp2(-5.0 - ret_decay.astype(F32)))
    max_seq = max(x_prompt.shape[1], x_sample.shape[1])
    inv_freq = ROPE_BASE ** (-jnp.arange(half, dtype=F32) / half)
    ang = jnp.arange(max_seq, dtype=F32)[:, None] * inv_freq[None, :]
    cos = jnp.cos(ang)
    sin = jnp.sin(ang)
    y_prompt = _trunk(x_prompt, p_prompt, params, log_gamma, cos, sin)
    y_sample = _trunk(x_sample, p_sample, params, log_gamma, cos, sin)
    return (y_prompt, y_sample)
```

```python
import functools

import jax
import jax.numpy as jnp
from jax import lax
from jax.experimental import pallas as pl
from jax.experimental.pallas import tpu as pltpu

RET_HEADS = 4
SG_GROUPS = 8
SG_CHUNK = 128
ROPE_BASE = 10000.0
EPS = 1e-6

RET_TILE = 256
ROW_CHUNK = 256
MXU_TILE = 256

F32 = jnp.float32
BF16 = jnp.bfloat16
V7X_VMEM_BYTES = 64 * 1024 * 1024


def _vmem_limit(estimate_bytes):
    return int(min(V7X_VMEM_BYTES - 6 * 1024 * 1024, estimate_bytes * 1.25 + 4 * 1024 * 1024))


def _tile(total, preferred):
    tile = min(total, preferred)
    assert total % tile == 0, (total, tile)
    return tile


def _rms(x):
    return x * lax.rsqrt(jnp.mean(x * x, axis=-1, keepdims=True) + EPS)


def _sigmoid(x):
    return 1.0 / (1.0 + jnp.exp(-x))


def _in_proj_kernel(h_ref, g_ref, w_ref, cos_ref, sin_ref, lng_ref, lnb_ref, o_ref, hn_ref,
                    *, head_dim, group_dim, row_chunk):
    tm = o_ref.shape[0]
    width = o_ref.shape[1] // 6
    half = head_dim // 2

    def rotary(acc, rows, base, scale):
        c = cos_ref[rows, :]
        s = sin_ref[rows, :]
        for hd in range(acc.shape[1] // head_dim):
            lo = hd * head_dim
            x1 = acc[:, lo:lo + half]
            x2 = acc[:, lo + half:lo + head_dim]
            o_ref[rows, base + lo:base + lo + half] = ((x1 * c - x2 * s) * scale).astype(BF16)
            o_ref[rows, base + lo + half:base + lo + head_dim] = ((x1 * s + x2 * c) * scale).astype(BF16)

    def group_norm(acc, rows, base):
        for g in range(acc.shape[1] // group_dim):
            lo = g * group_dim
            seg_lo = base % width + lo
            x = jax.nn.gelu(acc[:, lo:lo + group_dim])
            xc = x - jnp.mean(x, axis=-1, keepdims=True)
            y = xc * lax.rsqrt(jnp.mean(xc * xc, axis=-1, keepdims=True) + EPS)
            o_ref[rows, base + lo:base + lo + group_dim] = (
                y * lng_ref[:, seg_lo:seg_lo + group_dim] + lnb_ref[:, seg_lo:seg_lo + group_dim]
            ).astype(BF16)

    def plain(fn):
        def epilogue(acc, rows, base):
            o_ref[rows, base:base + acc.shape[1]] = fn(acc).astype(BF16)
        return epilogue

    epilogues = (
        lambda acc, rows, base: rotary(acc, rows, base, 1.0),
        lambda acc, rows, base: rotary(acc, rows, base, head_dim ** -0.5),
        plain(lambda acc: acc),
        plain(lambda acc: acc * _sigmoid(acc)),
        plain(jax.nn.gelu),
        group_norm,
    )

    def normalise(r):
        rows = pl.ds(r * row_chunk, row_chunk)
        hn_ref[rows, :] = (_rms(h_ref[rows, :]) * g_ref[...]).astype(BF16)

    order = (5, 4, 3, 0, 1, 2)
    col_tile = max(head_dim, MXU_TILE)
    n_chunks = tm // row_chunk
    normalise(0)
    for r in range(n_chunks):
        if r + 1 < n_chunks:
            normalise(r + 1)
        rows = pl.ds(r * row_chunk, row_chunk)
        hn = hn_ref[rows, :]
        for seg_index in order:
            for base in range(seg_index * width, (seg_index + 1) * width, col_tile):
                acc = jnp.dot(hn, w_ref[:, base:base + col_tile], preferred_element_type=F32)
                epilogues[seg_index](acc, rows, base)


def _in_proj(h, gain, w, cos, sin, ln_g, ln_b, *, layer, seq, tm):
    t, d = h.shape
    n = w.shape[2]
    seg = n // 6
    head_dim = seg // RET_HEADS
    group_dim = seg // SG_GROUPS
    pos_blocks = seq // tm
    row_chunk = _tile(tm, ROW_CHUNK)
    est = (2 * tm * d * 4 + d * n * 2 + 2 * tm * n * 2 + tm * d * 2
           + 4 * tm * (head_dim // 2) * 4 + 6 * row_chunk * seg * 4)
    return pl.pallas_call(
        functools.partial(_in_proj_kernel, head_dim=head_dim, group_dim=group_dim, row_chunk=row_chunk),
        grid=(t // tm,),
        in_specs=[
            pl.BlockSpec((tm, d), lambda i: (i, 0)),
            pl.BlockSpec((None, 1, d), lambda i: (layer, 0, 0)),
            pl.BlockSpec((None, d, n), lambda i: (layer, 0, 0), pipeline_mode=pl.Buffered(1)),
            pl.BlockSpec((tm, head_dim // 2), lambda i: (i % pos_blocks, 0)),
            pl.BlockSpec((tm, head_dim // 2), lambda i: (i % pos_blocks, 0)),
            pl.BlockSpec((None, 1, seg), lambda i: (layer, 0, 0)),
            pl.BlockSpec((None, 1, seg), lambda i: (layer, 0, 0)),
        ],
        out_specs=pl.BlockSpec((tm, n), lambda i: (i, 0)),
        out_shape=jax.ShapeDtypeStruct((t, n), BF16),
        scratch_shapes=[pltpu.VMEM((tm, d), BF16)],
        compiler_params=pltpu.CompilerParams(
            dimension_semantics=("arbitrary",), vmem_limit_bytes=_vmem_limit(est)),
        name="in_proj",
    )(h, gain, w, cos, sin, ln_g, ln_b)


def _state_update(s_old, k, v, k_dec, c_dec):
    kd = (k.astype(F32) * k_dec).astype(BF16)
    u = lax.dot_general(kd, v, (((0,), (0,)), ((), ())), preferred_element_type=F32)
    return s_old * c_dec + u


def _ret_state_kernel(lg_ref, k_ref, v_ref, sb_ref, st_ref, *, layer, head_dim, chunk):
    @pl.when(pl.program_id(1) == 0)
    def _():
        st_ref[...] = jnp.zeros_like(st_ref)

    pos = lax.broadcasted_iota(jnp.int32, (chunk, 1), 0).astype(F32)
    for c in reversed(range(k_ref.shape[0] // chunk)):
        rows = pl.ds(c * chunk, chunk)
        for hd in range(RET_HEADS):
            cols = pl.ds(hd * head_dim, head_dim)
            lgb = lg_ref[layer, 1, hd]
            s_old = st_ref[hd]
            sb_ref[c, hd] = s_old.astype(BF16)
            st_ref[hd] = _state_update(
                s_old, k_ref[rows, cols], v_ref[rows, cols], jnp.exp(lgb * pos),
                jnp.exp(jnp.full((1, head_dim), lgb * chunk, F32)))


def _ret_state(proj3, lg, *, layer, seg, tile):
    b, s, _ = proj3.shape
    c = RET_TILE
    per_tile = tile // c
    nt = s // tile
    head_dim = seg // RET_HEADS
    est = 4 * tile * seg * 2 + 2 * per_tile * RET_HEADS * head_dim * head_dim * 2 + RET_HEADS * head_dim * head_dim * 4
    return pl.pallas_call(
        functools.partial(_ret_state_kernel, layer=layer, head_dim=head_dim, chunk=c),
        grid=(b, nt),
        in_specs=[
            pl.BlockSpec(memory_space=pltpu.SMEM),
            pl.BlockSpec((None, tile, seg), lambda i, t: (i, nt - 1 - t, 1)),
            pl.BlockSpec((None, tile, seg), lambda i, t: (i, nt - 1 - t, 2)),
        ],
        out_specs=pl.BlockSpec((None, per_tile, RET_HEADS, head_dim, head_dim),
                               lambda i, t: (i, nt - 1 - t, 0, 0, 0)),
        out_shape=jax.ShapeDtypeStruct((b, s // c, RET_HEADS, head_dim, head_dim), BF16),
        scratch_shapes=[pltpu.VMEM((RET_HEADS, head_dim, head_dim), F32)],
        compiler_params=pltpu.CompilerParams(
            dimension_semantics=("arbitrary", "arbitrary"), vmem_limit_bytes=_vmem_limit(est)),
        name="ret_state",
    )(lg, proj3, proj3)


def _mix_out_kernel(lg_ref, p_ref, sb_ref, h_ref, sgw_ref, sgb_ref, wo_ref, o_ref, y_ref, sf_ref,
                    *, layer, seg, head_dim, group_dim, chunk):
    tile = p_ref.shape[0]

    @pl.when(pl.program_id(1) == 0)
    def _():
        sf_ref[...] = jnp.zeros_like(sf_ref)

    row = lax.broadcasted_iota(jnp.int32, (chunk, chunk), 0)
    col = lax.broadcasted_iota(jnp.int32, (chunk, chunk), 1)
    dist = (row - col).astype(F32)
    causal = row >= col
    pos = lax.broadcasted_iota(jnp.int32, (chunk, 1), 0).astype(F32)

    for c in range(tile // chunk):
        rows = pl.ds(c * chunk, chunk)
        for hd in range(RET_HEADS):
            lo = hd * head_dim
            lgf = lg_ref[layer, 0, hd]
            lgb = lg_ref[layer, 1, hd]
            q = p_ref[rows, pl.ds(lo, head_dim)]
            k = p_ref[rows, pl.ds(seg + lo, head_dim)]
            v = p_ref[rows, pl.ds(2 * seg + lo, head_dim)]
            gate = p_ref[rows, pl.ds(3 * seg + lo, head_dim)]
            s_fwd = sf_ref[hd]
            scores = lax.dot_general(q, k, (((1,), (1,)), ((), ())), preferred_element_type=F32)
            decay = jnp.exp(jnp.where(causal, lgf, -lgb) * dist)
            qf = q.astype(F32)
            lhs = jnp.concatenate([
                (scores * decay).astype(BF16),
                (qf * jnp.exp(lgf * (pos + 1.0))).astype(BF16),
                (qf * jnp.exp(lgb * (chunk - pos))).astype(BF16)], axis=1)
            rhs = jnp.concatenate([v, s_fwd.astype(BF16), sb_ref[c, hd]], axis=0)
            y = jnp.dot(lhs, rhs, preferred_element_type=F32)
            y_ref[rows, pl.ds(lo, head_dim)] = (gate.astype(F32) * _rms(y)).astype(BF16)
            sf_ref[hd] = _state_update(
                s_fwd, k, v, jnp.exp(lgf * (chunk - 1.0 - pos)),
                jnp.exp(jnp.full((1, head_dim), lgf * chunk, F32)))

    n_sub = tile // SG_CHUNK
    for g in range(SG_GROUPS):
        lo = g * group_dim
        vcat = jnp.concatenate(
            [p_ref[pl.ds(n * SG_CHUNK, SG_CHUNK), pl.ds(5 * seg + lo, group_dim)] for n in range(n_sub)],
            axis=1)
        z = jnp.dot(sgw_ref[g], vcat, preferred_element_type=F32)
        bias = sgb_ref[:, lo:lo + group_dim]
        for n in range(n_sub):
            sub = pl.ds(n * SG_CHUNK, SG_CHUNK)
            zn = z[:, n * group_dim:(n + 1) * group_dim] + bias
            un = p_ref[sub, pl.ds(4 * seg + lo, group_dim)].astype(F32)
            y_ref[sub, pl.ds(seg + lo, group_dim)] = (un * zn).astype(BF16)

    o_ref[...] = h_ref[...] + jnp.dot(y_ref[...], wo_ref[...], preferred_element_type=F32)


def _mix_out(proj3, sb, h3, lg, sg_w, sg_bias, w_out, *, layer, seg, tile):
    b, s, n = proj3.shape
    d = h3.shape[-1]
    c = RET_TILE
    per_tile = tile // c
    head_dim = seg // RET_HEADS
    group_dim = seg // SG_GROUPS
    state_bytes = RET_HEADS * head_dim * head_dim
    est = (2 * tile * n * 2 + 2 * per_tile * state_bytes * 2 + 4 * tile * d * 4 + d * d * 2
           + 2 * SG_CHUNK * seg * 4 + tile * d * 2 + state_bytes * 4 + 8 * c * c * 4)
    return pl.pallas_call(
        functools.partial(_mix_out_kernel, layer=layer, seg=seg, head_dim=head_dim,
                          group_dim=group_dim, chunk=c),
        grid=(b, s // tile),
        in_specs=[
            pl.BlockSpec(memory_space=pltpu.SMEM),
            pl.BlockSpec((None, tile, n), lambda i, t: (i, t, 0)),
            pl.BlockSpec((None, per_tile, RET_HEADS, head_dim, head_dim), lambda i, t: (i, t, 0, 0, 0)),
            pl.BlockSpec((None, tile, d), lambda i, t: (i, t, 0)),
            pl.BlockSpec((None, SG_GROUPS, SG_CHUNK, SG_CHUNK), lambda i, t: (layer, 0, 0, 0)),
            pl.BlockSpec((None, SG_CHUNK, seg), lambda i, t: (layer, 0, 0)),
            pl.BlockSpec((None, d, d), lambda i, t: (layer, 0, 0), pipeline_mode=pl.Buffered(1)),
        ],
        out_specs=pl.BlockSpec((None, tile, d), lambda i, t: (i, t, 0)),
        out_shape=jax.ShapeDtypeStruct((b, s, d), F32),
        scratch_shapes=[pltpu.VMEM((tile, d), BF16),
                        pltpu.VMEM((RET_HEADS, head_dim, head_dim), F32)],
        compiler_params=pltpu.CompilerParams(
            dimension_semantics=("arbitrary", "arbitrary"), vmem_limit_bytes=_vmem_limit(est)),
        name="mix_out",
    )(lg, proj3, sb, h3, sg_w, sg_bias, w_out)


def _ffn_kernel(h_ref, g_ref, wg_ref, wu_ref, wd_ref, o_ref, hn_ref, *, row_chunk):
    def swiglu_down(hn):
        gate = jnp.dot(hn, wg_ref[...], preferred_element_type=F32)
        up = jnp.dot(hn, wu_ref[...], preferred_element_type=F32)
        act = (gate * _sigmoid(gate) * up).astype(BF16)
        return jnp.dot(act, wd_ref[...], preferred_element_type=F32)

    @pl.when(pl.program_id(1) == 0)
    def _():
        for r in range(h_ref.shape[0] // row_chunk):
            rows = pl.ds(r * row_chunk, row_chunk)
            x = h_ref[rows, :]
            hn = (_rms(x) * g_ref[...]).astype(BF16)
            hn_ref[rows, :] = hn
            o_ref[rows, :] = x + swiglu_down(hn)

    @pl.when(pl.program_id(1) > 0)
    def _():
        o_ref[...] += swiglu_down(hn_ref[...])


def _ffn(h, gain, w_gate, w_up, w_down, *, layer, tm, tf):
    t, d = h.shape
    dff = w_gate.shape[2]
    est = 4 * tm * d * 4 + tm * d * 2 + 2 * 3 * d * tf * 2 + 3 * tm * tf * 4
    return pl.pallas_call(
        functools.partial(_ffn_kernel, row_chunk=_tile(tm, ROW_CHUNK)),
        grid=(t // tm, dff // tf),
        in_specs=[
            pl.BlockSpec((tm, d), lambda i, j: (i, 0)),
            pl.BlockSpec((None, 1, d), lambda i, j: (layer, 0, 0)),
            pl.BlockSpec((None, d, tf), lambda i, j: (layer, 0, j)),
            pl.BlockSpec((None, d, tf), lambda i, j: (layer, 0, j)),
            pl.BlockSpec((None, tf, d), lambda i, j: (layer, j, 0)),
        ],
        out_specs=pl.BlockSpec((tm, d), lambda i, j: (i, 0)),
        out_shape=jax.ShapeDtypeStruct((t, d), F32),
        scratch_shapes=[pltpu.VMEM((tm, d), BF16)],
        compiler_params=pltpu.CompilerParams(
            dimension_semantics=("arbitrary", "arbitrary"), vmem_limit_bytes=_vmem_limit(est)),
        name="ffn",
    )(h, gain, w_gate, w_up, w_down)


def _ple_kernel(h_ref, p_ref, g_ref, wg_ref, wp_ref, gf_ref, o_ref, hn_ref, *, final, row_chunk):
    def normalise(r):
        rows = pl.ds(r * row_chunk, row_chunk)
        hn_ref[rows, :] = (_rms(h_ref[rows, :]) * g_ref[...]).astype(BF16)

    n_chunks = h_ref.shape[0] // row_chunk
    normalise(0)
    for r in range(n_chunks):
        if r + 1 < n_chunks:
            normalise(r + 1)
        rows = pl.ds(r * row_chunk, row_chunk)
        hn = hn_ref[rows, :]
        pe = p_ref[rows, :].astype(BF16)
        for lo in range(0, o_ref.shape[1], MXU_TILE):
            cols = slice(lo, lo + MXU_TILE)
            gate = _sigmoid(jnp.dot(hn, wg_ref[:, cols], preferred_element_type=F32))
            emb = jnp.dot(pe, wp_ref[:, cols], preferred_element_type=F32)
            o_ref[rows, cols] = h_ref[rows, cols] + emb * gate
        if final:
            o_ref[rows, :] = _rms(o_ref[rows, :]) * gf_ref[...]


def _ple(h, p, gain, w_gate, w_proj, gain_final, *, layer, tm, final):
    t, d = h.shape
    pd = p.shape[2]
    row_chunk = _tile(tm, ROW_CHUNK)
    est = (4 * tm * d * 4 + 2 * tm * pd * 4 + d * d * 2 + pd * d * 2 + tm * d * 2
           + 4 * row_chunk * d * 4)
    return pl.pallas_call(
        functools.partial(_ple_kernel, final=final, row_chunk=row_chunk),
        grid=(t // tm,),
        in_specs=[
            pl.BlockSpec((tm, d), lambda i: (i, 0)),
            pl.BlockSpec((None, tm, pd), lambda i: (layer, i, 0)),
            pl.BlockSpec((None, 1, d), lambda i: (layer, 0, 0)),
            pl.BlockSpec((None, d, d), lambda i: (layer, 0, 0), pipeline_mode=pl.Buffered(1)),
            pl.BlockSpec((None, pd, d), lambda i: (layer, 0, 0), pipeline_mode=pl.Buffered(1)),
            pl.BlockSpec((1, d), lambda i: (0, 0)),
        ],
        out_specs=pl.BlockSpec((tm, d), lambda i: (i, 0)),
        out_shape=jax.ShapeDtypeStruct((t, d), F32),
        scratch_shapes=[pltpu.VMEM((tm, d), BF16)],
        compiler_params=pltpu.CompilerParams(
            dimension_semantics=("arbitrary",), vmem_limit_bytes=_vmem_limit(est)),
        name="ple",
    )(h, p, gain, w_gate, w_proj, gain_final)


def _trunk(x, p, params, log_gamma, cos, sin):
    b, s, d = x.shape
    depth = p.shape[0]
    t = b * s
    seg = params["w_in"].shape[2] // 6
    assert s % RET_TILE == 0 and RET_TILE % SG_CHUNK == 0

    mix_tile = _tile(s, 2 * RET_TILE)
    assert mix_tile % RET_TILE == 0
    tm_proj = _tile(s, 512)
    tm_ffn = _tile(t, 1024)
    tf = _tile(params["w_ffn_gate"].shape[2], 512)
    tm_ple = _tile(t, 1024)

    h = x.reshape(t, d)
    p = p.reshape(depth, t, -1)
    for l in range(depth):
        proj = _in_proj(h, params["norm_mix_g"], params["w_in"], cos, sin,
                        params["sg_ln_g"], params["sg_ln_b"], layer=l, seq=s, tm=tm_proj)
        proj3 = proj.reshape(b, s, 6 * seg)
        sb = _ret_state(proj3, log_gamma, layer=l, seg=seg, tile=mix_tile)
        h3 = _mix_out(proj3, sb, h.reshape(b, s, d), log_gamma, params["sg_w"],
                      params["sg_bias"], params["w_out"], layer=l, seg=seg, tile=mix_tile)
        h = _ffn(h3.reshape(t, d), params["norm_ffn_g"], params["w_ffn_gate"],
                 params["w_ffn_up"], params["w_ffn_down"], layer=l, tm=tm_ffn, tf=tf)
        h = _ple(h, p, params["norm_ple_g"], params["w_ple_gate"], params["w_ple_proj"],
                 params["norm_final_g"], layer=l, tm=tm_ple, final=(l == depth - 1))
    return h.reshape(b, s, d)


def kernel(x_prompt, x_sample, p_prompt, p_sample, norm_mix_g, w_in, ret_decay, sg_ln_g, sg_ln_b,
           sg_w, sg_b, w_out, norm_ffn_g, w_ffn_gate, w_ffn_up, w_ffn_down, norm_ple_g, w_ple_gate,
           w_ple_proj, norm_final_g):
    depth, d = norm_mix_g.shape
    seg = w_in.shape[2] // 6
    group_dim = seg // SG_GROUPS
    half = seg // RET_HEADS // 2
    params = {
        "norm_mix_g": norm_mix_g.reshape(depth, 1, d),
        "w_in": w_in.astype(BF16),
        "sg_ln_g": sg_ln_g.reshape(depth, 1, seg),
        "sg_ln_b": sg_ln_b.reshape(depth, 1, seg),
        "sg_w": sg_w.astype(BF16),
        "sg_bias": jnp.repeat(jnp.swapaxes(sg_b, 1, 2), group_dim, axis=2),
        "w_out": w_out.astype(BF16),
        "norm_ffn_g": norm_ffn_g.reshape(depth, 1, d),
        "w_ffn_gate": w_ffn_gate.astype(BF16),
        "w_ffn_up": w_ffn_up.astype(BF16),
        "w_ffn_down": w_ffn_down.astype(BF16),
        "norm_ple_g": norm_ple_g.reshape(depth, 1, d),
        "w_ple_gate": w_ple_gate.astype(BF16),
        "w_ple_proj": w_ple_proj.astype(BF16),
        "norm_final_g": norm_final_g.reshape(1, d),
    }
    log_gamma = jnp.log1p(-jnp.exp2(-5.0 - ret_decay.astype(F32)))
    max_seq = max(x_prompt.shape[1], x_sample.shape[1])
    inv_freq = ROPE_BASE ** (-jnp.arange(half, dtype=F32) / half)
    ang = jnp.arange(max_seq, dtype=F32)[:, None] * inv_freq[None, :]
    cos = jnp.cos(ang)
    sin = jnp.sin(ang)
    y_prompt = _trunk(x_prompt, p_prompt, params, log_gamma, cos, sin)
    y_sample = _trunk(x_sample, p_sample, params, log_gamma, cos, sin)
    return (y_prompt, y_sample)
```

```python
import functools

import jax
import jax.numpy as jnp
from jax import lax
from jax.experimental import pallas as pl
from jax.experimental.pallas import tpu as pltpu

RET_HEADS = 4
SG_GROUPS = 8
SG_CHUNK = 128
ROPE_BASE = 10000.0
EPS = 1e-6

RET_TILE = 256
ROW_CHUNK = 256
MXU_TILE = 256

F32 = jnp.float32
BF16 = jnp.bfloat16
V7X_VMEM_BYTES = 64 * 1024 * 1024


def _vmem_limit(estimate_bytes):
    return int(min(V7X_VMEM_BYTES - 6 * 1024 * 1024, estimate_bytes * 1.25 + 4 * 1024 * 1024))


def _params(n_grid_axes, estimate_bytes):
    return pltpu.CompilerParams(
        dimension_semantics=("arbitrary",) * n_grid_axes,
        vmem_limit_bytes=_vmem_limit(estimate_bytes))


def _tile(total, preferred):
    tile = min(total, preferred)
    assert total % tile == 0, (total, tile)
    return tile


def _rms(x):
    return x * lax.rsqrt(jnp.mean(x * x, axis=-1, keepdims=True) + EPS)


def _sigmoid(x):
    return 1.0 / (1.0 + jnp.exp(-x))


def _in_proj_kernel(h_ref, g_ref, w_ref, cos_ref, sin_ref, lng_ref, lnb_ref, o_ref, hn_ref,
                    *, head_dim, group_dim, row_chunk):
    tm = o_ref.shape[0]
    width = o_ref.shape[1] // 6
    half = head_dim // 2

    def rotary(acc, rows, base, scale):
        c = cos_ref[rows, :]
        s = sin_ref[rows, :]
        for hd in range(acc.shape[1] // head_dim):
            lo = hd * head_dim
            x1 = acc[:, lo:lo + half]
            x2 = acc[:, lo + half:lo + head_dim]
            o_ref[rows, base + lo:base + lo + half] = ((x1 * c - x2 * s) * scale).astype(BF16)
            o_ref[rows, base + lo + half:base + lo + head_dim] = ((x1 * s + x2 * c) * scale).astype(BF16)

    def group_norm(acc, rows, base):
        for g in range(acc.shape[1] // group_dim):
            lo = g * group_dim
            seg_lo = base % width + lo
            x = jax.nn.gelu(acc[:, lo:lo + group_dim])
            xc = x - jnp.mean(x, axis=-1, keepdims=True)
            y = xc * lax.rsqrt(jnp.mean(xc * xc, axis=-1, keepdims=True) + EPS)
            o_ref[rows, base + lo:base + lo + group_dim] = (
                y * lng_ref[:, seg_lo:seg_lo + group_dim] + lnb_ref[:, seg_lo:seg_lo + group_dim]
            ).astype(BF16)

    def plain(fn):
        def epilogue(acc, rows, base):
            o_ref[rows, base:base + acc.shape[1]] = fn(acc).astype(BF16)
        return epilogue

    epilogues = (
        lambda acc, rows, base: rotary(acc, rows, base, 1.0),
        lambda acc, rows, base: rotary(acc, rows, base, head_dim ** -0.5),
        plain(lambda acc: acc),
        plain(lambda acc: acc * _sigmoid(acc)),
        plain(jax.nn.gelu),
        group_norm,
    )

    def normalise(r):
        rows = pl.ds(r * row_chunk, row_chunk)
        hn_ref[rows, :] = (_rms(h_ref[rows, :]) * g_ref[...]).astype(BF16)

    order = (5, 4, 3, 0, 1, 2)
    col_tile = max(head_dim, MXU_TILE)
    n_chunks = tm // row_chunk
    normalise(0)
    for r in range(n_chunks):
        if r + 1 < n_chunks:
            normalise(r + 1)
        rows = pl.ds(r * row_chunk, row_chunk)
        hn = hn_ref[rows, :]
        for seg_index in order:
            for base in range(seg_index * width, (seg_index + 1) * width, col_tile):
                acc = jnp.dot(hn, w_ref[:, base:base + col_tile], preferred_element_type=F32)
                epilogues[seg_index](acc, rows, base)


def _in_proj(h, gain, w, cos, sin, ln_g, ln_b, *, layer, seq, tm):
    t, d = h.shape
    n = w.shape[2]
    seg = n // 6
    head_dim = seg // RET_HEADS
    group_dim = seg // SG_GROUPS
    pos_blocks = seq // tm
    row_chunk = _tile(tm, ROW_CHUNK)
    est = (2 * tm * d * 4 + d * n * 2 + 2 * tm * n * 2 + tm * d * 2
           + 4 * tm * (head_dim // 2) * 4 + 6 * row_chunk * seg * 4)
    return pl.pallas_call(
        functools.partial(_in_proj_kernel, head_dim=head_dim, group_dim=group_dim, row_chunk=row_chunk),
        grid=(t // tm,),
        in_specs=[
            pl.BlockSpec((tm, d), lambda i: (i, 0)),
            pl.BlockSpec((None, 1, d), lambda i: (layer, 0, 0)),
            pl.BlockSpec((None, d, n), lambda i: (layer, 0, 0), pipeline_mode=pl.Buffered(1)),
            pl.BlockSpec((tm, head_dim // 2), lambda i: (i % pos_blocks, 0)),
            pl.BlockSpec((tm, head_dim // 2), lambda i: (i % pos_blocks, 0)),
            pl.BlockSpec((None, 1, seg), lambda i: (layer, 0, 0)),
            pl.BlockSpec((None, 1, seg), lambda i: (layer, 0, 0)),
        ],
        out_specs=pl.BlockSpec((tm, n), lambda i: (i, 0)),
        out_shape=jax.ShapeDtypeStruct((t, n), BF16),
        scratch_shapes=[pltpu.VMEM((tm, d), BF16)],
        compiler_params=_params(1, est),
        name="in_proj",
    )(h, gain, w, cos, sin, ln_g, ln_b)


def _state_update(s_old, k, v, k_dec, c_dec):
    kd = (k.astype(F32) * k_dec).astype(BF16)
    u = lax.dot_general(kd, v, (((0,), (0,)), ((), ())), preferred_element_type=F32)
    return s_old * c_dec + u


def _ret_state_kernel(lg_ref, k_ref, v_ref, sb_ref, st_ref, *, layer, head_dim, chunk):
    @pl.when(pl.program_id(1) == 0)
    def _():
        st_ref[...] = jnp.zeros_like(st_ref)

    pos = lax.broadcasted_iota(jnp.int32, (chunk, 1), 0).astype(F32)
    for c in reversed(range(k_ref.shape[0] // chunk)):
        rows = pl.ds(c * chunk, chunk)
        for hd in range(RET_HEADS):
            cols = pl.ds(hd * head_dim, head_dim)
            lgb = lg_ref[layer, 1, hd]
            s_old = st_ref[hd]
            sb_ref[c, hd] = s_old.astype(BF16)
            st_ref[hd] = _state_update(
                s_old, k_ref[rows, cols], v_ref[rows, cols], jnp.exp(lgb * pos),
                jnp.exp(jnp.full((1, head_dim), lgb * chunk, F32)))


def _ret_state(proj3, lg, *, layer, seg, tile):
    b, s, _ = proj3.shape
    c = RET_TILE
    per_tile = tile // c
    nt = s // tile
    head_dim = seg // RET_HEADS
    est = 4 * tile * seg * 2 + 2 * per_tile * RET_HEADS * head_dim * head_dim * 2 + RET_HEADS * head_dim * head_dim * 4
    return pl.pallas_call(
        functools.partial(_ret_state_kernel, layer=layer, head_dim=head_dim, chunk=c),
        grid=(b, nt),
        in_specs=[
            pl.BlockSpec(memory_space=pltpu.SMEM),
            pl.BlockSpec((None, tile, seg), lambda i, t: (i, nt - 1 - t, 1)),
            pl.BlockSpec((None, tile, seg), lambda i, t: (i, nt - 1 - t, 2)),
        ],
        out_specs=pl.BlockSpec((None, per_tile, RET_HEADS, head_dim, head_dim),
                               lambda i, t: (i, nt - 1 - t, 0, 0, 0)),
        out_shape=jax.ShapeDtypeStruct((b, s // c, RET_HEADS, head_dim, head_dim), BF16),
        scratch_shapes=[pltpu.VMEM((RET_HEADS, head_dim, head_dim), F32)],
        compiler_params=_params(2, est),
        name="ret_state",
    )(lg, proj3, proj3)


def _mix_out_kernel(lg_ref, p_ref, sb_ref, h_ref, sgw_ref, sgb_ref, wo_ref, o_ref, y_ref, sf_ref,
                    sfc_ref, *, layer, seg, head_dim, group_dim, chunk):
    tile = p_ref.shape[0]
    n_chunks = tile // chunk

    @pl.when(pl.program_id(1) == 0)
    def _():
        sf_ref[...] = jnp.zeros_like(sf_ref)

    row = lax.broadcasted_iota(jnp.int32, (chunk, chunk), 0)
    col = lax.broadcasted_iota(jnp.int32, (chunk, chunk), 1)
    dist = (row - col).astype(F32)
    causal = row >= col
    pos = lax.broadcasted_iota(jnp.int32, (chunk, 1), 0).astype(F32)

    def head_cols(segment, hd):
        return pl.ds(segment * seg + hd * head_dim, head_dim)

    for hd in range(RET_HEADS):
        lgf = lg_ref[layer, 0, hd]
        k_dec = jnp.exp(lgf * (chunk - 1.0 - pos))
        c_dec = jnp.exp(jnp.full((1, head_dim), lgf * chunk, F32))
        state = sf_ref[hd]
        for c in range(n_chunks):
            rows = pl.ds(c * chunk, chunk)
            sfc_ref[c, hd] = state.astype(BF16)
            state = _state_update(state, p_ref[rows, head_cols(1, hd)], p_ref[rows, head_cols(2, hd)],
                                  k_dec, c_dec)
        sf_ref[hd] = state

    n_sub = tile // SG_CHUNK
    for g in range(SG_GROUPS):
        lo = g * group_dim
        vcat = jnp.concatenate(
            [p_ref[pl.ds(n * SG_CHUNK, SG_CHUNK), pl.ds(5 * seg + lo, group_dim)] for n in range(n_sub)],
            axis=1)
        z = jnp.dot(sgw_ref[g], vcat, preferred_element_type=F32)
        bias = sgb_ref[:, lo:lo + group_dim]
        for n in range(n_sub):
            sub = pl.ds(n * SG_CHUNK, SG_CHUNK)
            zn = z[:, n * group_dim:(n + 1) * group_dim] + bias
            un = p_ref[sub, pl.ds(4 * seg + lo, group_dim)].astype(F32)
            y_ref[sub, pl.ds(seg + lo, group_dim)] = (un * zn).astype(BF16)

    for c in range(n_chunks):
        rows = pl.ds(c * chunk, chunk)
        for hd in range(RET_HEADS):
            lgf = lg_ref[layer, 0, hd]
            lgb = lg_ref[layer, 1, hd]
            q = p_ref[rows, head_cols(0, hd)]
            scores = lax.dot_general(q, p_ref[rows, head_cols(1, hd)], (((1,), (1,)), ((), ())),
                                     preferred_element_type=F32)
            decay = jnp.exp(jnp.where(causal, lgf, -lgb) * dist)
            qf = q.astype(F32)
            lhs = jnp.concatenate([
                (scores * decay).astype(BF16),
                (qf * jnp.exp(lgf * (pos + 1.0))).astype(BF16),
                (qf * jnp.exp(lgb * (chunk - pos))).astype(BF16)], axis=1)
            rhs = jnp.concatenate([p_ref[rows, head_cols(2, hd)], sfc_ref[c, hd], sb_ref[c, hd]], axis=0)
            y = jnp.dot(lhs, rhs, preferred_element_type=F32)
            gate = p_ref[rows, head_cols(3, hd)].astype(F32)
            y_ref[rows, pl.ds(hd * head_dim, head_dim)] = (gate * _rms(y)).astype(BF16)
        o_ref[rows, :] = h_ref[rows, :] + jnp.dot(y_ref[rows, :], wo_ref[...], preferred_element_type=F32)


def _mix_out(proj3, sb, h3, lg, sg_w, sg_bias, w_out, *, layer, seg, tile):
    b, s, n = proj3.shape
    d = h3.shape[-1]
    c = RET_TILE
    per_tile = tile // c
    head_dim = seg // RET_HEADS
    group_dim = seg // SG_GROUPS
    state_bytes = RET_HEADS * head_dim * head_dim
    est = (2 * tile * n * 2 + 3 * per_tile * state_bytes * 2 + 4 * tile * d * 4 + d * d * 2
           + 2 * SG_CHUNK * seg * 4 + tile * d * 2 + state_bytes * 4 + 8 * c * c * 4)
    return pl.pallas_call(
        functools.partial(_mix_out_kernel, layer=layer, seg=seg, head_dim=head_dim,
                          group_dim=group_dim, chunk=c),
        grid=(b, s // tile),
        in_specs=[
            pl.BlockSpec(memory_space=pltpu.SMEM),
            pl.BlockSpec((None, tile, n), lambda i, t: (i, t, 0)),
            pl.BlockSpec((None, per_tile, RET_HEADS, head_dim, head_dim), lambda i, t: (i, t, 0, 0, 0)),
            pl.BlockSpec((None, tile, d), lambda i, t: (i, t, 0)),
            pl.BlockSpec((None, SG_GROUPS, SG_CHUNK, SG_CHUNK), lambda i, t: (layer, 0, 0, 0)),
            pl.BlockSpec((None, SG_CHUNK, seg), lambda i, t: (layer, 0, 0)),
            pl.BlockSpec((None, d, d), lambda i, t: (layer, 0, 0), pipeline_mode=pl.Buffered(1)),
        ],
        out_specs=pl.BlockSpec((None, tile, d), lambda i, t: (i, t, 0)),
        out_shape=jax.ShapeDtypeStruct((b, s, d), F32),
        scratch_shapes=[pltpu.VMEM((tile, d), BF16),
                        pltpu.VMEM((RET_HEADS, head_dim, head_dim), F32),
                        pltpu.VMEM((per_tile, RET_HEADS, head_dim, head_dim), BF16)],
        compiler_params=_params(2, est),
        name="mix_out",
    )(lg, proj3, sb, h3, sg_w, sg_bias, w_out)


def _ffn_kernel(h_ref, g_ref, wg_ref, wu_ref, wd_ref, o_ref, hn_ref, *, row_chunk):
    def swiglu_down(hn):
        gate = jnp.dot(hn, wg_ref[...], preferred_element_type=F32)
        up = jnp.dot(hn, wu_ref[...], preferred_element_type=F32)
        act = (gate * _sigmoid(gate) * up).astype(BF16)
        return jnp.dot(act, wd_ref[...], preferred_element_type=F32)

    @pl.when(pl.program_id(1) == 0)
    def _():
        for r in range(h_ref.shape[0] // row_chunk):
            rows = pl.ds(r * row_chunk, row_chunk)
            x = h_ref[rows, :]
            hn = (_rms(x) * g_ref[...]).astype(BF16)
            hn_ref[rows, :] = hn
            o_ref[rows, :] = x + swiglu_down(hn)

    @pl.when(pl.program_id(1) > 0)
    def _():
        o_ref[...] += swiglu_down(hn_ref[...])


def _ffn(h, gain, w_gate, w_up, w_down, *, layer, tm, tf):
    t, d = h.shape
    dff = w_gate.shape[2]
    est = 4 * tm * d * 4 + tm * d * 2 + 2 * 3 * d * tf * 2 + 3 * tm * tf * 4
    return pl.pallas_call(
        functools.partial(_ffn_kernel, row_chunk=_tile(tm, ROW_CHUNK)),
        grid=(t // tm, dff // tf),
        in_specs=[
            pl.BlockSpec((tm, d), lambda i, j: (i, 0)),
            pl.BlockSpec((None, 1, d), lambda i, j: (layer, 0, 0)),
            pl.BlockSpec((None, d, tf), lambda i, j: (layer, 0, j)),
            pl.BlockSpec((None, d, tf), lambda i, j: (layer, 0, j)),
            pl.BlockSpec((None, tf, d), lambda i, j: (layer, j, 0)),
        ],
        out_specs=pl.BlockSpec((tm, d), lambda i, j: (i, 0)),
        out_shape=jax.ShapeDtypeStruct((t, d), F32),
        scratch_shapes=[pltpu.VMEM((tm, d), BF16)],
        compiler_params=_params(2, est),
        name="ffn",
    )(h, gain, w_gate, w_up, w_down)


def _ple_kernel(h_ref, p_ref, g_ref, wg_ref, wp_ref, gf_ref, o_ref, hn_ref, *, final, row_chunk):
    def normalise(r):
        rows = pl.ds(r * row_chunk, row_chunk)
        hn_ref[rows, :] = (_rms(h_ref[rows, :]) * g_ref[...]).astype(BF16)

    n_chunks = h_ref.shape[0] // row_chunk
    normalise(0)
    for r in range(n_chunks):
        if r + 1 < n_chunks:
            normalise(r + 1)
        rows = pl.ds(r * row_chunk, row_chunk)
        hn = hn_ref[rows, :]
        pe = p_ref[rows, :].astype(BF16)
        for lo in range(0, o_ref.shape[1], MXU_TILE):
            cols = slice(lo, lo + MXU_TILE)
            gate = _sigmoid(jnp.dot(hn, wg_ref[:, cols], preferred_element_type=F32))
            emb = jnp.dot(pe, wp_ref[:, cols], preferred_element_type=F32)
            o_ref[rows, cols] = h_ref[rows, cols] + emb * gate
        if final:
            o_ref[rows, :] = _rms(o_ref[rows, :]) * gf_ref[...]


def _ple(h, p, gain, w_gate, w_proj, gain_final, *, layer, tm, final):
    t, d = h.shape
    pd = p.shape[2]
    row_chunk = _tile(tm, ROW_CHUNK)
    est = (4 * tm * d * 4 + 2 * tm * pd * 4 + d * d * 2 + pd * d * 2 + tm * d * 2
           + 4 * row_chunk * d * 4)
    return pl.pallas_call(
        functools.partial(_ple_kernel, final=final, row_chunk=row_chunk),
        grid=(t // tm,),
        in_specs=[
            pl.BlockSpec((tm, d), lambda i: (i, 0)),
            pl.BlockSpec((None, tm, pd), lambda i: (layer, i, 0)),
            pl.BlockSpec((None, 1, d), lambda i: (layer, 0, 0)),
            pl.BlockSpec((None, d, d), lambda i: (layer, 0, 0), pipeline_mode=pl.Buffered(1)),
            pl.BlockSpec((None, pd, d), lambda i: (layer, 0, 0), pipeline_mode=pl.Buffered(1)),
            pl.BlockSpec((1, d), lambda i: (0, 0)),
        ],
        out_specs=pl.BlockSpec((tm, d), lambda i: (i, 0)),
        out_shape=jax.ShapeDtypeStruct((t, d), F32),
        scratch_shapes=[pltpu.VMEM((tm, d), BF16)],
        compiler_params=_params(1, est),
        name="ple",
    )(h, p, gain, w_gate, w_proj, gain_final)


def _trunk(x, p, params, log_gamma, cos, sin):
    b, s, d = x.shape
    depth = p.shape[0]
    t = b * s
    seg = params["w_in"].shape[2] // 6
    assert s % RET_TILE == 0 and RET_TILE % SG_CHUNK == 0

    mix_tile = _tile(s, 2 * RET_TILE)
    assert mix_tile % RET_TILE == 0
    tm_proj = _tile(s, 512)
    tm_ffn = _tile(t, 1024)
    tf = _tile(params["w_ffn_gate"].shape[2], 512)
    tm_ple = _tile(t, 1024)

    h = x.reshape(t, d)
    p = p.reshape(depth, t, -1)
    for l in range(depth):
        proj = _in_proj(h, params["norm_mix_g"], params["w_in"], cos, sin,
                        params["sg_ln_g"], params["sg_ln_b"], layer=l, seq=s, tm=tm_proj)
        proj3 = proj.reshape(b, s, 6 * seg)
        sb = _ret_state(proj3, log_gamma, layer=l, seg=seg, tile=mix_tile)
        h3 = _mix_out(proj3, sb, h.reshape(b, s, d), log_gamma, params["sg_w"],
                      params["sg_bias"], params["w_out"], layer=l, seg=seg, tile=mix_tile)
        h = _ffn(h3.reshape(t, d), params["norm_ffn_g"], params["w_ffn_gate"],
                 params["w_ffn_up"], params["w_ffn_down"], layer=l, tm=tm_ffn, tf=tf)
        h = _ple(h, p, params["norm_ple_g"], params["w_ple_gate"], params["w_ple_proj"],
                 params["norm_final_g"], layer=l, tm=tm_ple, final=(l == depth - 1))
    return h.reshape(b, s, d)


def kernel(x_prompt, x_sample, p_prompt, p_sample, norm_mix_g, w_in, ret_decay, sg_ln_g, sg_ln_b,
           sg_w, sg_b, w_out, norm_ffn_g, w_ffn_gate, w_ffn_up, w_ffn_down, norm_ple_g, w_ple_gate,
           w_ple_proj, norm_final_g):
    depth, d = norm_mix_g.shape
    seg = w_in.shape[2] // 6
    group_dim = seg // SG_GROUPS
    half = seg // RET_HEADS // 2
    params = {
        "norm_mix_g": norm_mix_g.reshape(depth, 1, d),
        "w_in": w_in.astype(BF16),
        "sg_ln_g": sg_ln_g.reshape(depth, 1, seg),
        "sg_ln_b": sg_ln_b.reshape(depth, 1, seg),
        "sg_w": sg_w.astype(BF16),
        "sg_bias": jnp.repeat(jnp.swapaxes(sg_b, 1, 2), group_dim, axis=2),
        "w_out": w_out.astype(BF16),
        "norm_ffn_g": norm_ffn_g.reshape(depth, 1, d),
        "w_ffn_gate": w_ffn_gate.astype(BF16),
        "w_ffn_up": w_ffn_up.astype(BF16),
        "w_ffn_down": w_ffn_down.astype(BF16),
        "norm_ple_g": norm_ple_g.reshape(depth, 1, d),
        "w_ple_gate": w_ple_gate.astype(BF16),
        "w_ple_proj": w_ple_proj.astype(BF16),
        "norm_final_g": norm_final_g.reshape(1, d),
    }
    log_gamma = jnp.log1p(-jnp.exp2(-5.0 - ret_decay.astype(F32)))
    max_seq = max(x_prompt.shape[1], x_sample.shape[1])
    inv_freq = ROPE_BASE ** (-jnp.arange(half, dtype=F32) / half)
    ang = jnp.arange(max_seq, dtype=F32)[:, None] * inv_freq[None, :]
    cos = jnp.cos(ang)
    sin = jnp.sin(ang)
    y_prompt = _trunk(x_prompt, p_prompt, params, log_gamma, cos, sin)
    y_sample = _trunk(x_sample, p_sample, params, log_gamma, cos, sin)
    return (y_prompt, y_sample)
```
